```python
import math
import jax
import jax.numpy as jnp
from jax import lax
import numpy as np

D_MODEL = 2048
BATCH = 16
SEQ = 2048
DEPTH = 2
DEC_BATCH = 2
DEC_SEQ = 8192
PAST_LEN = 128

N_MIXERS = 2
N_HYENA_LAYERS = (DEPTH + 1) // 2
N_DIFF_LAYERS = DEPTH // 2
HYENA_ORDER = 2
SHORT_CONV = 3
FILTER_EMB = 33
FILTER_BANDS = (FILTER_EMB - 1) // 2
FILTER_HIDDEN = 64
FILTER_INNER = 2
FAST_DECAY_PCT = 0.3
SLOW_DECAY_PCT = 1.5
DECAY_TARGET = 1e-2
MAX_DECAY = math.log(DECAY_TARGET) / FAST_DECAY_PCT
MIN_DECAY = math.log(DECAY_TARGET) / SLOW_DECAY_PCT
DIFF_HEAD_DIM = 128
N_DIFF_HEADS = D_MODEL // (2 * DIFF_HEAD_DIM)
ROPE_DIM = DIFF_HEAD_DIM // 4
ROPE_THETA = 500000.0
Q_BLOCK = 128
N_EXPERTS = 32
TOP_K = 4
D_EXPERT = D_MODEL
SWIGLU_ALPHA = 1.702
SWIGLU_LIMIT = 7.0
MOE_BLOCK = 256
LN_EPS = 1e-5
ALPHA = (2 * DEPTH) ** 0.25
BETA = (8 * DEPTH) ** -0.25

kernel_name = 'hyena_diffattn_moe_deepnorm_encoder'


def layer_norm(x, g, b):
    xf = x.astype(jnp.float32)
    mu = jnp.mean(xf, axis=-1, keepdims=True)
    var = jnp.mean(jnp.square(xf - mu), axis=-1, keepdims=True)
    y = (xf - mu) * lax.rsqrt(var + LN_EPS) * g.astype(jnp.float32) + b.astype(jnp.float32)
    return y.astype(x.dtype)


def centred_short_conv(u, w, b):
    pad = SHORT_CONV // 2
    L = u.shape[1]
    up = jnp.pad(u, ((0, 0), (pad, pad), (0, 0)))
    y = b
    for j in range(SHORT_CONV):
        y = y + up[:, j:j + L] * w[j]
    return y


def hyena_filter_spectra(L, w_in, b_in, w_mid, b_mid, freq, w_out):
    f32 = jnp.float32
    t = jnp.linspace(0.0, 1.0, L, dtype=f32)[:, None]
    omega = (2.0 * math.pi / L) * jnp.arange(L, dtype=f32)
    bands = jnp.linspace(1e-4, FILTER_BANDS - 1, FILTER_BANDS, dtype=f32)
    phase = omega[:, None] * bands[None, :]
    z = jnp.concatenate([t, jnp.cos(phase), -jnp.sin(phase)], axis=-1)
    freq = freq.astype(f32)
    h = jnp.sin(freq[0] * (z @ w_in.astype(f32) + b_in.astype(f32)))
    for j in range(FILTER_INNER):
        h = jnp.sin(freq[j + 1] * (h @ w_mid[j].astype(f32) + b_mid[j].astype(f32)))
    h = (h @ w_out.astype(f32)).reshape(L, HYENA_ORDER, 2, D_MODEL)
    deltas = jnp.abs(jnp.linspace(MIN_DECAY, MAX_DECAY, D_MODEL, dtype=f32))
    h = h * jnp.exp(-t * deltas[None, :])[:, None, None, :]
    h_fwd, h_bwd = h[:, :, 0], h[:, :, 1]
    k = jnp.concatenate([h_fwd, jnp.zeros((1, HYENA_ORDER, D_MODEL), f32), h_bwd[1:][::-1]], axis=0)
    k = k / jnp.sum(jnp.abs(k), axis=0, keepdims=True)
    return jnp.fft.rfft(k, axis=0)


def bidirectional_long_conv(z, k_hat):
    L = z.shape[1]
    zf = jnp.fft.rfft(z, n=2 * L, axis=1)
    return jnp.fft.irfft(zf * k_hat[None], n=2 * L, axis=1)[:, :L]


def hyena_mixer(x, w_in, b_in, conv_w, conv_b, f_w_in, f_b_in, f_w_mid, f_b_mid, f_freq,
                f_w_out, f_skip, w_out, b_out):
    L = x.shape[1]
    u = centred_short_conv(x @ w_in + b_in, conv_w, conv_b)
    parts = jnp.split(u.astype(jnp.float32), HYENA_ORDER + 1, axis=-1)
    k_hat = hyena_filter_spectra(L, f_w_in, f_b_in, f_w_mid, f_b_mid, f_freq, f_w_out)
    skip = f_skip.astype(jnp.float32)
    z = parts[0]
    for n in range(HYENA_ORDER):
        z = parts[n + 1] * (bidirectional_long_conv(z, k_hat[:, n]) + skip[n] * z)
    return z.astype(x.dtype) @ w_out + b_out


def partial_rope(x):
    L = x.shape[1]
    half = ROPE_DIM // 2
    inv_freq = ROPE_THETA ** (-jnp.arange(0, ROPE_DIM, 2, dtype=jnp.float32) / ROPE_DIM)
    ang = jnp.arange(L, dtype=jnp.float32)[:, None] * inv_freq[None, :]
    cos = jnp.cos(ang)[None, :, None, :]
    sin = jnp.sin(ang)[None, :, None, :]
    xr = x[..., :ROPE_DIM].astype(jnp.float32)
    x1, x2 = xr[..., :half], xr[..., half:]
    rot = jnp.concatenate([x1 * cos - x2 * sin, x2 * cos + x1 * sin], axis=-1)
    return jnp.concatenate([rot.astype(x.dtype), x[..., ROPE_DIM:]], axis=-1)


def diff_attention(x, w_qkv, lq1, lk1, lq2, lk2, subln_g, w_o, layer_idx):
    B, L, D = x.shape
    H, dh = N_DIFF_HEADS, DIFF_HEAD_DIM
    f32 = jnp.float32
    n_blocks = L // Q_BLOCK
    q, k, v = jnp.split(x @ w_qkv, 3, axis=-1)
    q = partial_rope(q.reshape(B, L, 2 * H, dh)).reshape(B, L, H, 2, dh) * (dh ** -0.5)
    k = partial_rope(k.reshape(B, L, 2 * H, dh)).reshape(B, L, H, 2, dh)
    v = v.reshape(B, L, H, 2 * dh)
    lam_init = 0.8 - 0.6 * math.exp(-0.3 * layer_idx)
    lam = (jnp.exp(jnp.sum(lq1.astype(f32) * lk1.astype(f32)))
           - jnp.exp(jnp.sum(lq2.astype(f32) * lk2.astype(f32))) + lam_init)
    g = subln_g.astype(f32)
    q_blocks = jnp.moveaxis(q.reshape(B, n_blocks, Q_BLOCK, H, 2, dh), 1, 0)

    def attend(qb):
        s = jnp.einsum('bqhcd,bkhcd->bhcqk', qb, k, preferred_element_type=f32)
        p = jax.nn.softmax(s, axis=-1)
        a = p[:, :, 0] - lam * p[:, :, 1]
        o = jnp.einsum('bhqk,bkhe->bqhe', a.astype(v.dtype), v, preferred_element_type=f32)
        o = o * lax.rsqrt(jnp.mean(jnp.square(o), axis=-1, keepdims=True) + LN_EPS) * g
        return (o * (1.0 - lam_init)).astype(x.dtype)

    o = lax.map(attend, q_blocks)
    o = jnp.moveaxis(o, 0, 1).reshape(B, L, H * 2 * dh)
    return o @ w_o


def clamped_swiglu(gate, up):
    gate = jnp.minimum(gate, SWIGLU_LIMIT)
    up = jnp.clip(up, -SWIGLU_LIMIT, SWIGLU_LIMIT)
    return gate * jax.nn.sigmoid(SWIGLU_ALPHA * gate) * (up + 1.0)


def moe_ffn(x, router_w, router_b, w_gate, b_gate, w_up, b_up, w_down, b_down):
    B, L, D = x.shape
    T = B * L
    A = T * TOP_K
    n_blocks = (A + N_EXPERTS * (MOE_BLOCK - 1) + MOE_BLOCK - 1) // MOE_BLOCK
    P = n_blocks * MOE_BLOCK
    xt = x.reshape(T, D)
    logits = jnp.matmul(xt, router_w, preferred_element_type=jnp.float32) + router_b.astype(jnp.float32)
    top_logits, top_idx = lax.top_k(logits, TOP_K)
    gates = jax.nn.softmax(top_logits, axis=-1)
    flat_e = top_idx.reshape(A)
    order = jnp.argsort(flat_e)
    sorted_e = flat_e[order]
    counts = jnp.bincount(flat_e, length=N_EXPERTS)
    padded = (counts + MOE_BLOCK - 1) // MOE_BLOCK * MOE_BLOCK
    start = jnp.cumsum(counts) - counts
    padded_end = jnp.cumsum(padded)
    padded_start = padded_end - padded
    dest = padded_start[sorted_e] + (jnp.arange(A, dtype=jnp.int32) - start[sorted_e])
    row_token = jnp.full((P,), T, jnp.int32).at[dest].set((order // TOP_K).astype(jnp.int32))
    row_gate = jnp.zeros((P,), jnp.float32).at[dest].set(gates.reshape(A)[order])
    block_start = jnp.arange(n_blocks, dtype=jnp.int32) * MOE_BLOCK
    block_expert = jnp.minimum(jnp.searchsorted(padded_end, block_start, side='right'), N_EXPERTS - 1)
    x_pad = jnp.concatenate([xt, jnp.zeros((1, D), xt.dtype)], axis=0)

    def expert_block(args):
        tok, e = args
        xb = x_pad[tok]
        h = clamped_swiglu(xb @ w_gate[e] + b_gate[e], xb @ w_up[e] + b_up[e])
        return h @ w_down[e] + b_down[e]

    y = lax.map(expert_block, (row_token.reshape(n_blocks, MOE_BLOCK), block_expert))
    y = y.reshape(P, D).astype(jnp.float32) * row_gate[:, None]
    out = jax.ops.segment_sum(y, row_token, num_segments=T + 1)[:T]
    return out.astype(x.dtype).reshape(B, L, D)


def encoder(x, p):
    for i in range(DEPTH):
        j = i // N_MIXERS
        if i % N_MIXERS == 0:
            h = hyena_mixer(x, p['hy_w_in'][j], p['hy_b_in'][j], p['hy_conv_w'][j], p['hy_conv_b'][j],
                            p['hy_filt_w_in'][j], p['hy_filt_b_in'][j], p['hy_filt_w_mid'][j],
                            p['hy_filt_b_mid'][j], p['hy_filt_freq'][j], p['hy_filt_w_out'][j],
                            p['hy_filt_skip'][j], p['hy_w_out'][j], p['hy_b_out'][j])
        else:
            h = diff_attention(x, p['da_w_qkv'][j], p['da_lambda_q1'][j], p['da_lambda_k1'][j],
                               p['da_lambda_q2'][j], p['da_lambda_k2'][j], p['da_subln_g'][j],
                               p['da_w_o'][j], i)
        x = layer_norm(ALPHA * x + h, p['ln1_g'][i], p['ln1_b'][i])
        m = moe_ffn(x, p['moe_router_w'][i], p['moe_router_b'][i], p['moe_w_gate'][i], p['moe_b_gate'][i],
                    p['moe_w_up'][i], p['moe_b_up'][i], p['moe_w_down'][i], p['moe_b_down'][i])
        x = layer_norm(ALPHA * x + m, p['ln2_g'][i], p['ln2_b'][i])
    return x


def setup_inputs(seed: int = 0) -> dict:
    key = jax.random.key(seed)
    keys = iter(jax.random.split(key, 48))

    def nrm(shape, scale):
        return scale * jax.random.normal(next(keys), shape, jnp.float32)

    D, F, E = D_MODEL, D_EXPERT, N_EXPERTS
    NH, NA = N_HYENA_LAYERS, N_DIFF_LAYERS
    PR = HYENA_ORDER + 1
    return {
        'x_prompt': nrm((BATCH, SEQ, D), 1.0),
        'x_sample': nrm((DEC_BATCH, DEC_SEQ, D), 1.0),
        'hy_w_in': nrm((NH, D, PR * D), D ** -0.5),
        'hy_b_in': nrm((NH, PR * D), 0.02),
        'hy_conv_w': nrm((NH, SHORT_CONV, PR * D), SHORT_CONV ** -0.5),
        'hy_conv_b': nrm((NH, PR * D), 0.02),
        'hy_filt_w_in': nrm((NH, FILTER_EMB, FILTER_HIDDEN), FILTER_EMB ** -0.5),
        'hy_filt_b_in': nrm((NH, FILTER_HIDDEN), 0.1),
        'hy_filt_w_mid': nrm((NH, FILTER_INNER, FILTER_HIDDEN, FILTER_HIDDEN), FILTER_HIDDEN ** -0.5),
        'hy_filt_b_mid': nrm((NH, FILTER_INNER, FILTER_HIDDEN), 0.1),
        'hy_filt_freq': 1.0 + nrm((NH, FILTER_INNER + 1, FILTER_HIDDEN), 0.1),
        'hy_filt_w_out': nrm((NH, FILTER_HIDDEN, HYENA_ORDER * 2 * D), FILTER_HIDDEN ** -0.5),
        'hy_filt_skip': nrm((NH, HYENA_ORDER, D), 0.1),
        'hy_w_out': nrm((NH, D, D), BETA * D ** -0.5),
        'hy_b_out': nrm((NH, D), 0.02),
        'da_w_qkv': jnp.concatenate([nrm((NA, D, 2 * D), D ** -0.5),
                                     nrm((NA, D, D), BETA * D ** -0.5)], axis=-1),
        'da_lambda_q1': nrm((NA, DIFF_HEAD_DIM), 0.1),
        'da_lambda_k1': nrm((NA, DIFF_HEAD_DIM), 0.1),
        'da_lambda_q2': nrm((NA, DIFF_HEAD_DIM), 0.1),
        'da_lambda_k2': nrm((NA, DIFF_HEAD_DIM), 0.1),
        'da_subln_g': 1.0 + nrm((NA, 2 * DIFF_HEAD_DIM), 0.01),
        'da_w_o': nrm((NA, D, D), BETA * D ** -0.5),
        'ln1_g': 1.0 + nrm((DEPTH, D), 0.01),
        'ln1_b': nrm((DEPTH, D), 0.01),
        'moe_router_w': nrm((DEPTH, D, E), D ** -0.5),
        'moe_router_b': nrm((DEPTH, E), 0.01),
        'moe_w_gate': nrm((DEPTH, E, D, F), D ** -0.5),
        'moe_b_gate': nrm((DEPTH, E, F), 0.02),
        'moe_w_up': nrm((DEPTH, E, D, F), D ** -0.5),
        'moe_b_up': nrm((DEPTH, E, F), 0.02),
        'moe_w_down': nrm((DEPTH, E, F, D), BETA * F ** -0.5),
        'moe_b_down': nrm((DEPTH, E, D), 0.02),
        'ln2_g': 1.0 + nrm((DEPTH, D), 0.01),
        'ln2_b': nrm((DEPTH, D), 0.01),
    }


def reference(x_prompt, x_sample, hy_w_in, hy_b_in, hy_conv_w, hy_conv_b, hy_filt_w_in, hy_filt_b_in,
              hy_filt_w_mid, hy_filt_b_mid, hy_filt_freq, hy_filt_w_out, hy_filt_skip, hy_w_out, hy_b_out,
              da_w_qkv, da_lambda_q1, da_lambda_k1, da_lambda_q2, da_lambda_k2, da_subln_g, da_w_o,
              ln1_g, ln1_b, moe_router_w, moe_router_b, moe_w_gate, moe_b_gate, moe_w_up, moe_b_up,
              moe_w_down, moe_b_down, ln2_g, ln2_b):
    p = dict(hy_w_in=hy_w_in, hy_b_in=hy_b_in, hy_conv_w=hy_conv_w, hy_conv_b=hy_conv_b,
             hy_filt_w_in=hy_filt_w_in, hy_filt_b_in=hy_filt_b_in, hy_filt_w_mid=hy_filt_w_mid,
             hy_filt_b_mid=hy_filt_b_mid, hy_filt_freq=hy_filt_freq, hy_filt_w_out=hy_filt_w_out,
             hy_filt_skip=hy_filt_skip, hy_w_out=hy_w_out, hy_b_out=hy_b_out,
             da_w_qkv=da_w_qkv, da_lambda_q1=da_lambda_q1, da_lambda_k1=da_lambda_k1,
             da_lambda_q2=da_lambda_q2, da_lambda_k2=da_lambda_k2, da_subln_g=da_subln_g, da_w_o=da_w_o,
             ln1_g=ln1_g, ln1_b=ln1_b, moe_router_w=moe_router_w, moe_router_b=moe_router_b,
             moe_w_gate=moe_w_gate, moe_b_gate=moe_b_gate, moe_w_up=moe_w_up, moe_b_up=moe_b_up,
             moe_w_down=moe_w_down, moe_b_down=moe_b_down, ln2_g=ln2_g, ln2_b=ln2_b)
    y_prompt = encoder(x_prompt, p)
    y_sample = encoder(x_sample, p)
    return (y_prompt, y_sample)
```

```python
import functools
import math

import jax
import jax.numpy as jnp
from jax import lax
from jax.experimental import pallas as pl
from jax.experimental.pallas import tpu as pltpu

DEPTH = 2
SHORT_CONV = 3
HYENA_ORDER = 2
FILTER_EMB = 33
FILTER_BANDS = (FILTER_EMB - 1) // 2
FILTER_INNER = 2
FAST_DECAY_PCT = 0.3
SLOW_DECAY_PCT = 1.5
DECAY_TARGET = 1e-2
MAX_DECAY = math.log(DECAY_TARGET) / FAST_DECAY_PCT
MIN_DECAY = math.log(DECAY_TARGET) / SLOW_DECAY_PCT
DIFF_HEAD_DIM = 128
ROPE_DIM = DIFF_HEAD_DIM // 4
ROPE_THETA = 500000.0
TOP_K = 4
SWIGLU_ALPHA = 1.702
SWIGLU_LIMIT = 7.0
LN_EPS = 1e-5
ALPHA = (2 * DEPTH) ** 0.25

LANES = 128
SUBLANES = 8
VMEM_LIMIT_BYTES = 60 * 1024 * 1024

HYENA_BLOCK = 512

F32 = jnp.float32
BF16 = jnp.bfloat16


def _params(semantics, vmem=VMEM_LIMIT_BYTES):
    return pltpu.CompilerParams(dimension_semantics=semantics, vmem_limit_bytes=vmem)


def _tile(n, pref):
    t = min(n, pref)
    assert n % t == 0, (n, t)
    return t


def _split_bf16(a):
    hi = a.astype(BF16)
    lo = (a - hi.astype(F32)).astype(BF16)
    return hi, lo


def _dot3(a, b):
    ah, al = _split_bf16(a)
    bh, bl = _split_bf16(b)
    d = functools.partial(jnp.dot, preferred_element_type=F32)
    return d(ah, bh) + d(al, bh) + d(ah, bl)


def _layer_norm(y, g, b):
    mu = jnp.mean(y, axis=-1, keepdims=True)
    yc = y - mu
    var = jnp.mean(yc * yc, axis=-1, keepdims=True)
    return yc * lax.rsqrt(var + LN_EPS) * g + b


def _proj_kernel(x_ref, w_ref, b_ref, o_ref):
    acc = jnp.dot(x_ref[...].astype(BF16), w_ref[...], preferred_element_type=F32)
    o_ref[...] = (acc + b_ref[...]).astype(o_ref.dtype)


def _project(x, w, b, out_dtype, tm=512, tn=2048):
    T, K = x.shape
    N = w.shape[1]
    tm, tn = _tile(T, tm), _tile(N, tn)
    return pl.pallas_call(
        _proj_kernel,
        grid=(N // tn, T // tm),
        in_specs=[pl.BlockSpec((tm, K), lambda n, m: (m, 0)),
                  pl.BlockSpec((K, tn), lambda n, m: (0, n)),
                  pl.BlockSpec((1, tn), lambda n, m: (0, n))],
        out_specs=pl.BlockSpec((tm, tn), lambda n, m: (m, n)),
        out_shape=jax.ShapeDtypeStruct((T, N), out_dtype),
        compiler_params=_params(("arbitrary", "arbitrary")),
        name="project",
    )(x, w, b.reshape(1, N))


def _qkv_kernel(x_ref, w_ref, invf_ref, o_ref, *, tm, split, l1, l2, n_rope_slabs):
    n = pl.program_id(0)
    m = pl.program_id(1)
    acc = jnp.dot(x_ref[...].astype(BF16), w_ref[...], preferred_element_type=F32)
    tn = acc.shape[1]

    @pl.when(n >= n_rope_slabs)
    def _():
        o_ref[...] = acc.astype(o_ref.dtype)

    @pl.when(n < n_rope_slabs)
    def _():
        start = m * tm
        pos0 = jnp.where(start < split, lax.rem(start, l1), lax.rem(jnp.maximum(start - split, 0), l2))
        row = lax.broadcasted_iota(jnp.int32, (tm, LANES), 0)
        lane = lax.broadcasted_iota(jnp.int32, (tm, LANES), 1)
        ang = (pos0 + row).astype(F32) * invf_ref[...]
        cos, sin = jnp.cos(ang), jnp.sin(ang)
        half = ROPE_DIM // 2
        c_tab = jnp.where(lane < ROPE_DIM, cos, 1.0)
        s_lo = jnp.where(lane < half, -sin, 0.0)
        s_hi = jnp.where((lane >= half) & (lane < ROPE_DIM), sin, 0.0)
        scale = jnp.where(n == 0, DIFF_HEAD_DIM ** -0.5, 1.0)
        for h in range(tn // LANES):
            xh = acc[:, h * LANES:(h + 1) * LANES]
            rot = (xh * c_tab + pltpu.roll(xh, LANES - half, 1) * s_lo
                   + pltpu.roll(xh, half, 1) * s_hi)
            o_ref[:, h * LANES:(h + 1) * LANES] = (rot * scale).astype(o_ref.dtype)


def _qkv_project(x, w, split, l1, l2, tm=512):
    T, K = x.shape
    N = w.shape[1]
    tn = N // 3
    tm = _tile(math.gcd(l1, l2), tm)
    fi = jnp.arange(LANES, dtype=jnp.int32) % (ROPE_DIM // 2)
    inv_freq = ROPE_THETA ** (-(2 * fi).astype(F32) / ROPE_DIM)
    inv_freq = jnp.where(jnp.arange(LANES) < ROPE_DIM, inv_freq, 0.0).reshape(1, LANES)
    return pl.pallas_call(
        functools.partial(_qkv_kernel, tm=tm, split=split, l1=l1, l2=l2, n_rope_slabs=2),
        grid=(3, T // tm),
        in_specs=[pl.BlockSpec((tm, K), lambda n, m: (m, 0)),
                  pl.BlockSpec((K, tn), lambda n, m: (0, n)),
                  pl.BlockSpec((1, LANES), lambda n, m: (0, 0))],
        out_specs=pl.BlockSpec((tm, tn), lambda n, m: (m, n)),
        out_shape=jax.ShapeDtypeStruct((T, N), BF16),
        compiler_params=_params(("arbitrary", "arbitrary")),
        name="qkv_project",
    )(x, w, inv_freq)


def _pack_bf16_pairs(y):
    half = y.shape[1] // 2
    r = pltpu.bitcast(y.astype(BF16).astype(F32), jnp.uint32)
    return (r[:, :half] & jnp.uint32(0xFFFF0000)) | (r[:, half:] >> 16)


def _unpack_bf16_pairs(w):
    hi = pltpu.bitcast(w & jnp.uint32(0xFFFF0000), F32)
    lo = pltpu.bitcast(w << 16, F32)
    return jnp.concatenate([hi, lo], axis=1).astype(BF16)


def _out_ln_kernel(a_ref, w_ref, b_ref, x_ref, g_ref, beta_ref, o_ref, p_ref):
    h = jnp.dot(a_ref[...], w_ref[...], preferred_element_type=F32) + b_ref[...]
    y = _layer_norm(ALPHA * x_ref[...] + h, g_ref[...], beta_ref[...])
    o_ref[...] = y
    p_ref[...] = _pack_bf16_pairs(y)


def _out_project_ln(a, w, b, x, g, beta, tm=512):
    T, K = a.shape
    D = w.shape[1]
    tm = _tile(T, tm)
    row = lambda i: (i, 0)
    fixed = lambda i: (0, 0)
    return pl.pallas_call(
        _out_ln_kernel,
        grid=(T // tm,),
        in_specs=[pl.BlockSpec((tm, K), row), pl.BlockSpec((K, D), fixed), pl.BlockSpec((1, D), fixed),
                  pl.BlockSpec((tm, D), row), pl.BlockSpec((1, D), fixed), pl.BlockSpec((1, D), fixed)],
        out_specs=[pl.BlockSpec((tm, D), row), pl.BlockSpec((tm, D // 2), row)],
        out_shape=[jax.ShapeDtypeStruct((T, D), F32), jax.ShapeDtypeStruct((T, D // 2), jnp.uint32)],
        compiler_params=_params(("arbitrary",)),
        name="out_project_ln",
    )(a, w, b.reshape(1, D), x, g.reshape(1, D), beta.reshape(1, D))


def _filter_kernel(z_ref, w_in_ref, b_in_ref, w_mid_ref, b_mid_ref, freq_ref, w_out_ref, delta_ref,
                   k_ref, nrm_ref, *, tl):
    r = pl.program_id(1)
    z = z_ref[...]
    h = jnp.sin(freq_ref[0:1, :] * (_dot3(z, w_in_ref[...]) + b_in_ref[...]))
    for j in range(FILTER_INNER):
        h = jnp.sin(freq_ref[j + 1:j + 2, :] * (_dot3(h, w_mid_ref[j]) + b_mid_ref[j:j + 1, :]))
    out = _dot3(h, w_out_ref[...]) * jnp.exp(-z[:, 0:1] * delta_ref[...])
    row = lax.broadcasted_iota(jnp.int32, out.shape, 0) + r * tl
    out = jnp.where(row == 0, 0.0, out)
    k_ref[0] = out

    @pl.when(r == 0)
    def _():
        nrm_ref[...] = jnp.zeros_like(nrm_ref)

    nrm_ref[0] = nrm_ref[0] + jnp.sum(jnp.abs(out), axis=0, keepdims=True)


def _hyena_filters(L, D, w_in, b_in, w_mid, b_mid, freq, w_out, tl=256):
    t = jnp.linspace(0.0, 1.0, L, dtype=F32)[:, None]
    omega = (2.0 * math.pi / L) * jnp.arange(L, dtype=F32)
    bands = jnp.linspace(1e-4, FILTER_BANDS - 1, FILTER_BANDS, dtype=F32)
    phase = omega[:, None] * bands[None, :]
    z = jnp.concatenate([t, jnp.cos(phase), -jnp.sin(phase)], axis=-1)
    z_lag = jnp.concatenate([z[:1], z[1:][::-1], z], axis=0)
    deltas = jnp.abs(jnp.linspace(MIN_DECAY, MAX_DECAY, D, dtype=F32)).reshape(1, D)
    tl = _tile(L, tl)
    n_neg = L // tl
    hid = w_in.shape[1]
    fixed2 = lambda n, r: (0, 0)
    return pl.pallas_call(
        functools.partial(_filter_kernel, tl=tl),
        grid=(HYENA_ORDER, 2 * L // tl),
        in_specs=[pl.BlockSpec((tl, FILTER_EMB), lambda n, r: (r, 0)),
                  pl.BlockSpec((FILTER_EMB, hid), fixed2), pl.BlockSpec((1, hid), fixed2),
                  pl.BlockSpec((FILTER_INNER, hid, hid), lambda n, r: (0, 0, 0)),
                  pl.BlockSpec((FILTER_INNER, hid), fixed2),
                  pl.BlockSpec((FILTER_INNER + 1, hid), fixed2),
                  pl.BlockSpec((hid, D), lambda n, r: (0, 2 * n + (r < n_neg).astype(jnp.int32))),
                  pl.BlockSpec((1, D), fixed2)],
        out_specs=[pl.BlockSpec((1, tl, D), lambda n, r: (n, r, 0)),
                   pl.BlockSpec((1, SUBLANES, D), lambda n, r: (n, 0, 0))],
        out_shape=[jax.ShapeDtypeStruct((HYENA_ORDER, 2 * L, D), F32),
                   jax.ShapeDtypeStruct((HYENA_ORDER, SUBLANES, D), F32)],
        compiler_params=_params(("arbitrary", "arbitrary")),
        name="hyena_filter",
    )(z_lag, w_in, b_in.reshape(1, hid), w_mid, b_mid, freq, w_out, deltas)


def _dft_matrices(bk):
    nfp = (bk + 1 + SUBLANES - 1) // SUBLANES * SUBLANES
    f = jnp.arange(nfp, dtype=jnp.int32)[:, None]
    r = jnp.arange(bk, dtype=jnp.int32)[None, :]
    ang = (2.0 * math.pi / (2 * bk)) * ((f * r) % (2 * bk)).astype(F32)
    live = f <= bk
    cos = jnp.where(live, jnp.cos(ang), 0.0)
    sin = jnp.where(live, jnp.sin(ang), 0.0)
    fwd = jnp.concatenate([cos, -sin], axis=0)
    wgt = jnp.where((f == 0) | (f == bk), 1.0, 2.0) / (2 * bk)
    inv = jnp.concatenate([cos * wgt, -sin * wgt], axis=0).T
    return fwd.astype(BF16), inv.astype(BF16), nfp


def _spectra_kernel(k_ref, nrm_ref, f_ref, c_ref, a_prev, k_prev, *, nfp):
    s = pl.program_id(2)
    kb = k_ref[0]
    a = jnp.dot(f_ref[...], kb.astype(BF16), preferred_element_type=F32)

    @pl.when(s > 0)
    def _():
        row = lax.broadcasted_iota(jnp.int32, a.shape, 0)
        sign = jnp.where((row & 1) == 0, 1.0, -1.0)
        k0 = jnp.where(row < nfp, k_prev[0:1, :], 0.0)
        c_ref[0, 0] = (a + sign * (a_prev[...] - k0)) / nrm_ref[0, 0:1, :]

    a_prev[...] = a
    k_prev[...] = kb[0:SUBLANES, :]


def _hyena_spectra(kappa, nrm, fwd, bk, nfp, dt=256):
    order, two_l, D = kappa.shape
    n_blk = two_l // bk
    dt = _tile(D, dt)
    return pl.pallas_call(
        functools.partial(_spectra_kernel, nfp=nfp),
        grid=(order, D // dt, n_blk),
        in_specs=[pl.BlockSpec((1, bk, dt), lambda n, d, s: (n, s, d)),
                  pl.BlockSpec((1, SUBLANES, dt), lambda n, d, s: (n, 0, d)),
                  pl.BlockSpec((2 * nfp, bk), lambda n, d, s: (0, 0))],
        out_specs=pl.BlockSpec((1, 1, 2 * nfp, dt), lambda n, d, s: (n, jnp.maximum(s - 1, 0), 0, d)),
        out_shape=jax.ShapeDtypeStruct((order, n_blk - 1, 2 * nfp, D), F32),
        scratch_shapes=[pltpu.VMEM((2 * nfp, dt), F32), pltpu.VMEM((SUBLANES, dt), F32)],
        compiler_params=_params(("arbitrary", "arbitrary", "arbitrary")),
        name="hyena_spectra",
    )(kappa, nrm, fwd)


def _short_conv_block(src_ref, w_ref, b_ref, j, nb, bk):
    cur = src_ref[0, j * bk:(j + 1) * bk, :]
    row = lax.broadcasted_iota(jnp.int32, cur.shape, 0)
    if j > 0:
        before = src_ref[0, j * bk - SUBLANES:j * bk, :][SUBLANES - 1:SUBLANES, :]
    else:
        before = jnp.zeros_like(cur[0:1, :])
    if j < nb - 1:
        after = src_ref[0, (j + 1) * bk:(j + 1) * bk + SUBLANES, :][0:1, :]
    else:
        after = jnp.zeros_like(cur[0:1, :])
    up = jnp.where(row == 0, before, pltpu.roll(cur, 1, 0))
    dn = jnp.where(row == bk - 1, after, pltpu.roll(cur, bk - 1, 0))
    return b_ref[...] + up * w_ref[0:1, :] + cur * w_ref[1:2, :] + dn * w_ref[2:3, :]


def _long_conv_kernel(*refs, nb, bk, nfp, conv_input):
    if conv_input:
        (z_ref, zw_ref, zb_ref, g_ref, gw_ref, gb_ref, skip_ref, c_ref, fwd_ref, inv_ref,
         o_ref, zf_ref, y_ref) = refs
    else:
        (z_ref, g_ref, gw_ref, gb_ref, skip_ref, c_ref, fwd_ref, inv_ref, o_ref, zf_ref, y_ref) = refs

    def z_block(j):
        if conv_input:
            return _short_conv_block(z_ref, zw_ref, zb_ref, j, nb, bk)
        return z_ref[0, j * bk:(j + 1) * bk, :]

    for j in range(nb):
        zf_ref[j] = jnp.dot(fwd_ref[...], z_block(j).astype(BF16), preferred_element_type=F32)

    for i in range(nb):
        def chunk(c, carry, i=i):
            r0 = pl.multiple_of(c * SUBLANES, SUBLANES)
            re_rows, im_rows = pl.ds(r0, SUBLANES), pl.ds(nfp + r0, SUBLANES)
            acc_re = acc_im = None
            for j in range(nb):
                m = i - j + nb - 1
                c_re, c_im = c_ref[0, m, re_rows, :], c_ref[0, m, im_rows, :]
                z_re, z_im = zf_ref[j, re_rows, :], zf_ref[j, im_rows, :]
                t_re = c_re * z_re - c_im * z_im
                t_im = c_re * z_im + c_im * z_re
                acc_re = t_re if acc_re is None else acc_re + t_re
                acc_im = t_im if acc_im is None else acc_im + t_im
            y_ref[re_rows, :] = acc_re
            y_ref[im_rows, :] = acc_im
            return carry

        lax.fori_loop(0, nfp // SUBLANES, chunk, 0)
        y = jnp.dot(inv_ref[...], y_ref[...].astype(BF16), preferred_element_type=F32)
        gate = _short_conv_block(g_ref, gw_ref, gb_ref, i, nb, bk)
        o_ref[0, i * bk:(i + 1) * bk, :] = (gate * (y + skip_ref[...] * z_block(i))).astype(o_ref.dtype)


def _long_conv(u3, z3, order, spectra, fwd, inv, conv_w, conv_b, skip, *, b0, nbatch, bk, nfp, dt, out_dtype):
    _, L, three_d = u3.shape
    D = three_d // 3
    nb = L // bk
    n_d = D // dt
    conv_input = z3 is None
    slab = lambda k: (lambda d, b: (b0 + b, 0, k * n_d + d))
    cols = lambda k: (lambda d, b: (0, k * n_d + d))
    seq_block = (1, L, dt)
    own = lambda d, b: (b, 0, d)
    in_specs, args = [], []
    if conv_input:
        in_specs += [pl.BlockSpec(seq_block, slab(0)), pl.BlockSpec((SHORT_CONV, dt), cols(0)),
                     pl.BlockSpec((1, dt), cols(0))]
        args += [u3, conv_w, conv_b]
    else:
        in_specs += [pl.BlockSpec(seq_block, own)]
        args += [z3]
    in_specs += [pl.BlockSpec(seq_block, slab(order + 1)), pl.BlockSpec((SHORT_CONV, dt), cols(order + 1)),
                 pl.BlockSpec((1, dt), cols(order + 1)),
                 pl.BlockSpec((1, dt), lambda d, b: (0, d)),
                 pl.BlockSpec((1, 2 * nb - 1, 2 * nfp, dt), lambda d, b: (order, 0, 0, d),
                              pipeline_mode=pl.Buffered(1)),
                 pl.BlockSpec((2 * nfp, bk), lambda d, b: (0, 0)),
                 pl.BlockSpec((bk, 2 * nfp), lambda d, b: (0, 0))]
    args += [u3, conv_w, conv_b, skip[order:order + 1], spectra, fwd, inv]
    return pl.pallas_call(
        functools.partial(_long_conv_kernel, nb=nb, bk=bk, nfp=nfp, conv_input=conv_input),
        grid=(n_d, nbatch),
        in_specs=in_specs,
        out_specs=pl.BlockSpec(seq_block, own),
        out_shape=jax.ShapeDtypeStruct((nbatch, L, D), out_dtype),
        scratch_shapes=[pltpu.VMEM((nb, 2 * nfp, dt), F32), pltpu.VMEM((2 * nfp, dt), F32)],
        compiler_params=_params(("arbitrary", "arbitrary")),
        name="hyena_long_conv",
    )(*args)


def _hyena_mixer(x, groups, p):
    T, D = x.shape
    u = _project(x, p["hy_w_in"].astype(BF16), p["hy_b_in"], F32)
    conv_w, conv_b, skip = p["hy_conv_w"], p["hy_conv_b"].reshape(1, 3 * D), p["hy_filt_skip"]
    outs = []
    for first, nbatch, L in groups:
        bk = min(L, HYENA_BLOCK)
        dt = _tile(D, 256 if L <= 2048 else 128)
        fwd, inv, nfp = _dft_matrices(bk)
        kappa, nrm = _hyena_filters(L, D, p["hy_filt_w_in"], p["hy_filt_b_in"], p["hy_filt_w_mid"],
                                    p["hy_filt_b_mid"], p["hy_filt_freq"], p["hy_filt_w_out"])
        spectra = _hyena_spectra(kappa, nrm, fwd, bk, nfp)
        kw = dict(b0=first // L, nbatch=nbatch, bk=bk, nfp=nfp, dt=dt)
        u3 = u.reshape(T // L, L, 3 * D)
        z1 = _long_conv(u3, None, 0, spectra, fwd, inv, conv_w, conv_b, skip, out_dtype=F32, **kw)
        z2 = _long_conv(u3, z1, 1, spectra, fwd, inv, conv_w, conv_b, skip, out_dtype=BF16, **kw)
        outs.append(z2.reshape(nbatch * L, D))
    return jnp.concatenate(outs, axis=0)


def _attn_kernel(q_ref, k_ref, v_ref, lam_ref, g_ref, o_ref, *, lam_init):
    dh = DIFF_HEAD_DIM
    lam_dots = jnp.sum(lam_ref[0:2, :] * lam_ref[2:4, :], axis=-1, keepdims=True)
    lam = jnp.exp(lam_dots[0:1, :]) - jnp.exp(lam_dots[1:2, :]) + lam_init
    q, k = q_ref[...], k_ref[...]
    probs = []
    for c in range(2):
        s = lax.dot_general(q[:, c * dh:(c + 1) * dh], k[:, c * dh:(c + 1) * dh],
                            (((1,), (1,)), ((), ())), preferred_element_type=F32)
        e = jnp.exp(s - jnp.max(s, axis=-1, keepdims=True))
        probs.append((e, jnp.sum(e, axis=-1, keepdims=True)))
    (e0, l0), (e1, l1) = probs
    a = e0 * (1.0 / l0) - e1 * (lam / l1)
    o = jnp.dot(a.astype(BF16), v_ref[...], preferred_element_type=F32)
    o = o * lax.rsqrt(jnp.mean(o * o, axis=-1, keepdims=True) + LN_EPS) * g_ref[...]
    o_ref[...] = (o * (1.0 - lam_init)).astype(o_ref.dtype)


def _diff_attention(qkv, lam_vecs, subln_g, layer_idx, *, first, nbatch, L, tq):
    T, three_d = qkv.shape
    D = three_d // 3
    hw = 2 * DIFF_HEAD_DIM
    H = D // hw
    tq = _tile(L, tq)
    nq = L // tq
    q0, b0 = first // tq, first // L
    lam_init = 0.8 - 0.6 * math.exp(-0.3 * layer_idx)
    return pl.pallas_call(
        functools.partial(_attn_kernel, lam_init=lam_init),
        grid=(nbatch, H, nq),
        in_specs=[pl.BlockSpec((tq, hw), lambda b, h, i: (q0 + b * nq + i, h)),
                  pl.BlockSpec((L, hw), lambda b, h, i: (b0 + b, H + h)),
                  pl.BlockSpec((L, hw), lambda b, h, i: (b0 + b, 2 * H + h)),
                  pl.BlockSpec((4, DIFF_HEAD_DIM), lambda b, h, i: (0, 0)),
                  pl.BlockSpec((1, hw), lambda b, h, i: (0, 0))],
        out_specs=pl.BlockSpec((tq, hw), lambda b, h, i: (b * nq + i, h)),
        out_shape=jax.ShapeDtypeStruct((nbatch * L, D), BF16),
        compiler_params=_params(("arbitrary", "arbitrary", "arbitrary")),
        name="diff_attention",
    )(qkv, qkv, qkv, lam_vecs, subln_g.reshape(1, hw))


def _attention_mixer(x, groups, p, layer_idx):
    split = groups[1][0] if len(groups) > 1 else x.shape[0]
    l1, l2 = groups[0][2], groups[-1][2]
    qkv = _qkv_project(x, p["da_w_qkv"].astype(BF16), split, l1, l2)
    lam_vecs = jnp.stack([p["da_lambda_q1"], p["da_lambda_q2"], p["da_lambda_k1"], p["da_lambda_k2"]])
    outs = [_diff_attention(qkv, lam_vecs, p["da_subln_g"], layer_idx, first=first, nbatch=nbatch, L=L,
                            tq=1024 if L <= 2048 else 256)
            for first, nbatch, L in groups]
    return jnp.concatenate(outs, axis=0)


def _router_kernel(x_ref, w_ref, b_ref, idx_ref, gate_ref, rank_ref, cnt_ref, base_ref, *, n_experts):
    i = pl.program_id(0)

    @pl.when(i == 0)
    def _():
        base_ref[...] = jnp.zeros_like(base_ref)

    logits = _dot3(x_ref[...], w_ref[...]) + b_ref[...]
    tm = logits.shape[0]
    lane = lax.broadcasted_iota(jnp.int32, (tm, LANES), 1)
    neg_inf = jnp.float32(-jnp.inf)
    lane_f = lane.astype(F32)
    live = jnp.where(lane < n_experts, logits, neg_inf)
    vals, picks, hots = [], [], []
    for _ in range(TOP_K):
        best = jnp.max(live, axis=-1, keepdims=True)
        pick = jnp.min(jnp.where(live == best, lane_f, float(LANES)), axis=-1, keepdims=True)
        hot = lane_f == pick
        live = jnp.where(hot, neg_inf, live)
        vals.append(best)
        picks.append(pick.astype(jnp.int32))
        hots.append(hot)
    exps = [jnp.exp(v - vals[0]) for v in vals]
    denom = exps[0] + exps[1] + exps[2] + exps[3]
    chosen = jnp.zeros((tm, LANES), F32)
    for hot in hots:
        chosen = chosen + hot.astype(F32)
    r_i = lax.broadcasted_iota(jnp.int32, (tm, tm), 0)
    c_i = lax.broadcasted_iota(jnp.int32, (tm, tm), 1)
    earlier = jnp.where(r_i > c_i, 1.0, 0.0).astype(BF16)
    before = base_ref[0:1, :] + jnp.dot(earlier, chosen.astype(BF16), preferred_element_type=F32)
    idx_out = jnp.zeros((tm, LANES), jnp.int32)
    rank_out = jnp.zeros((tm, LANES), jnp.int32)
    gate_out = jnp.zeros((tm, LANES), F32)
    for k in range(TOP_K):
        rank_k = jnp.sum(jnp.where(hots[k], before, 0.0), axis=-1, keepdims=True).astype(jnp.int32)
        idx_out = jnp.where(lane == k, picks[k], idx_out)
        rank_out = jnp.where(lane == k, rank_k, rank_out)
        gate_out = jnp.where(lane == k, exps[k] / denom, gate_out)
    idx_ref[...] = idx_out
    rank_ref[...] = rank_out
    gate_ref[...] = gate_out
    base_ref[...] = base_ref[...] + jnp.sum(chosen, axis=0, keepdims=True)
    cnt_ref[...] = base_ref[...]


def _route(x, router_w, router_b, tm=512):
    T, D = x.shape
    E = router_w.shape[1]
    tm = _tile(T, tm)
    w = jnp.pad(router_w, ((0, 0), (0, LANES - E)))
    b = jnp.pad(router_b, (0, LANES - E)).reshape(1, LANES)
    row = lambda i: (i, 0)
    fixed = lambda i: (0, 0)
    wide = pl.BlockSpec((tm, LANES), row)
    return pl.pallas_call(
        functools.partial(_router_kernel, n_experts=E),
        grid=(T // tm,),
        in_specs=[pl.BlockSpec((tm, D), row), pl.BlockSpec((D, LANES), fixed), pl.BlockSpec((1, LANES), fixed)],
        out_specs=[wide, wide, wide, pl.BlockSpec((SUBLANES, LANES), fixed)],
        out_shape=[jax.ShapeDtypeStruct((T, LANES), jnp.int32), jax.ShapeDtypeStruct((T, LANES), F32),
                   jax.ShapeDtypeStruct((T, LANES), jnp.int32), jax.ShapeDtypeStruct((SUBLANES, LANES), F32)],
        scratch_shapes=[pltpu.VMEM((SUBLANES, LANES), F32)],
        compiler_params=_params(("arbitrary",)),
        name="moe_route",
    )(x, w, b)


def _row_copy(src, src_row, dst, dst_row, sem):
    return pltpu.make_async_copy(src.at[pl.ds(src_row, 1)], dst.at[pl.ds(dst_row, 1)], sem)


def _dispatch_kernel(dest_ref, x_hbm, xs_in, xs_out, sem, *, tm):
    del xs_in
    base = pl.program_id(0) * tm

    def issue(t, carry):
        for k in range(TOP_K):
            _row_copy(x_hbm, base + t, xs_out, dest_ref[t * TOP_K + k], sem).start()
        return carry

    def drain(t, carry):
        for k in range(TOP_K):
            _row_copy(x_hbm, 0, xs_out, 0, sem).wait()
        return carry

    lax.fori_loop(0, tm, issue, 0)
    lax.fori_loop(0, tm, drain, 0)


def _dispatch(x_packed, dest_flat, n_rows, tm=256):
    T, half = x_packed.shape
    tm = _tile(T, tm)
    return pl.pallas_call(
        functools.partial(_dispatch_kernel, tm=tm),
        grid=(T // tm,),
        in_specs=[pl.BlockSpec((tm * TOP_K,), lambda i: (i,), memory_space=pltpu.SMEM),
                  pl.BlockSpec(memory_space=pl.ANY), pl.BlockSpec(memory_space=pl.ANY)],
        out_specs=pl.BlockSpec(memory_space=pl.ANY),
        out_shape=jax.ShapeDtypeStruct((n_rows, half), jnp.uint32),
        scratch_shapes=[pltpu.SemaphoreType.DMA(())],
        input_output_aliases={2: 0},
        compiler_params=_params(("arbitrary",)),
        name="moe_dispatch",
    )(dest_flat, x_packed, jnp.zeros((n_rows, half), jnp.uint32))


def _expert_kernel(tile_expert_ref, n_used_ref, x_ref, wg_ref, bg_ref, wu_ref, bu_ref, wd_ref, bd_ref, o_ref):
    del tile_expert_ref
    i, f = pl.program_id(0), pl.program_id(1)

    @pl.when(i < n_used_ref[0])
    def _():
        x = _unpack_bf16_pairs(x_ref[...])
        gate = jnp.dot(x, wg_ref[0], preferred_element_type=F32) + bg_ref[0]
        up = jnp.dot(x, wu_ref[0], preferred_element_type=F32) + bu_ref[0]
        gate = jnp.minimum(gate, SWIGLU_LIMIT)
        up = jnp.clip(up, -SWIGLU_LIMIT, SWIGLU_LIMIT)
        h = gate * (1.0 / (1.0 + jnp.exp(-SWIGLU_ALPHA * gate))) * (up + 1.0)
        part = jnp.dot(h.astype(BF16), wd_ref[0], preferred_element_type=F32)

        @pl.when(f == 0)
        def _():
            o_ref[...] = part + bd_ref[0]

        @pl.when(f > 0)
        def _():
            o_ref[...] = o_ref[...] + part


def _experts(xs, tile_expert, n_used, wg, bg, wu, bu, wd, bd, rb, fc=512):
    P, half = xs.shape
    E, D, F = wg.shape
    fc = _tile(F, fc)
    n_tiles = P // rb
    w_in = lambda i, f, te, nu: (te[i], 0, f)
    return pl.pallas_call(
        _expert_kernel,
        grid_spec=pltpu.PrefetchScalarGridSpec(
            num_scalar_prefetch=2,
            grid=(n_tiles, F // fc),
            in_specs=[pl.BlockSpec((rb, half), lambda i, f, te, nu: (i, 0)),
                      pl.BlockSpec((1, D, fc), w_in), pl.BlockSpec((1, 1, fc), w_in),
                      pl.BlockSpec((1, D, fc), w_in), pl.BlockSpec((1, 1, fc), w_in),
                      pl.BlockSpec((1, fc, D), lambda i, f, te, nu: (te[i], f, 0)),
                      pl.BlockSpec((1, 1, D), lambda i, f, te, nu: (te[i], 0, 0))],
            out_specs=pl.BlockSpec((rb, D), lambda i, f, te, nu: (i, 0))),
        out_shape=jax.ShapeDtypeStruct((P, D), F32),
        compiler_params=_params(("arbitrary", "arbitrary")),
        name="moe_experts",
    )(tile_expert, n_used, xs, wg, bg.reshape(E, 1, F), wu, bu.reshape(E, 1, F), wd, bd.reshape(E, 1, D))


def _combine_kernel(dest_ref, gate_ref, x_ref, g_ref, beta_ref, ys_hbm, o_ref, rows, sem, *, tc):
    def issue(t, carry):
        for k in range(TOP_K):
            _row_copy(ys_hbm, dest_ref[t * TOP_K + k], rows.at[k], t, sem).start()
        return carry

    def drain(t, carry):
        for k in range(TOP_K):
            _row_copy(ys_hbm, 0, rows.at[k], 0, sem).wait()
        return carry

    lax.fori_loop(0, tc, issue, 0)
    lax.fori_loop(0, tc, drain, 0)
    gates = gate_ref[...]
    mix = gates[:, 0:1] * rows[0]
    for k in range(1, TOP_K):
        mix = mix + gates[:, k:k + 1] * rows[k]
    o_ref[...] = _layer_norm(ALPHA * x_ref[...] + mix, g_ref[...], beta_ref[...])


def _combine_ln(ys, dest_flat, gates, x, g, beta, tc=256):
    T, D = x.shape
    tc = _tile(T, tc)
    row = lambda i: (i, 0)
    fixed = lambda i: (0, 0)
    return pl.pallas_call(
        functools.partial(_combine_kernel, tc=tc),
        grid=(T // tc,),
        in_specs=[pl.BlockSpec((tc * TOP_K,), lambda i: (i,), memory_space=pltpu.SMEM),
                  pl.BlockSpec((tc, LANES), row), pl.BlockSpec((tc, D), row),
                  pl.BlockSpec((1, D), fixed), pl.BlockSpec((1, D), fixed),
                  pl.BlockSpec(memory_space=pl.ANY)],
        out_specs=pl.BlockSpec((tc, D), row),
        out_shape=jax.ShapeDtypeStruct((T, D), F32),
        scratch_shapes=[pltpu.VMEM((TOP_K, tc, D), F32), pltpu.SemaphoreType.DMA(())],
        compiler_params=_params(("arbitrary",)),
        name="moe_combine_ln",
    )(dest_flat, gates, x, g.reshape(1, D), beta.reshape(1, D), ys)


def _moe_ln(x, x_packed, p, i, rb):
    T, D = x.shape
    E = p["moe_router_w"].shape[-1]
    idx, gates, rank, counts = _route(x, p["moe_router_w"][i], p["moe_router_b"][i])
    counts = counts[0, :E].astype(jnp.int32)
    padded = (counts + rb - 1) // rb * rb
    padded_end = jnp.cumsum(padded)
    padded_start = padded_end - padded
    n_tiles = (T * TOP_K) // rb + E
    tile_expert = jnp.minimum(jnp.searchsorted(padded_end, jnp.arange(n_tiles, dtype=jnp.int32) * rb, side="right"),
                              E - 1).astype(jnp.int32)
    n_used = (padded_end[-1:] // rb).astype(jnp.int32)
    dest = (padded_start[idx[:, :TOP_K]] + rank[:, :TOP_K]).reshape(T * TOP_K)
    xs = _dispatch(x_packed, dest, n_tiles * rb)
    ys = _experts(xs, tile_expert, n_used, p["moe_w_gate"][i].astype(BF16), p["moe_b_gate"][i],
                  p["moe_w_up"][i].astype(BF16), p["moe_b_up"][i], p["moe_w_down"][i].astype(BF16),
                  p["moe_b_down"][i], rb)
    return _combine_ln(ys, dest, gates, x, p["ln2_g"][i], p["ln2_b"][i])


def _encoder(x, groups, p, rb):
    for i in range(DEPTH):
        j = i // 2
        if i % 2 == 0:
            pj = {k: v[j] for k, v in p.items() if k.startswith("hy_")}
            a = _hyena_mixer(x, groups, pj)
            w_o, b_o = pj["hy_w_out"], pj["hy_b_out"]
        else:
            pj = {k: v[j] for k, v in p.items() if k.startswith("da_")}
            a = _attention_mixer(x, groups, pj, i)
            w_o, b_o = pj["da_w_o"], jnp.zeros((x.shape[1],), F32)
        x, x_packed = _out_project_ln(a, w_o.astype(BF16), b_o, x, p["ln1_g"][i], p["ln1_b"][i])
        x = _moe_ln(x, x_packed, p, i, rb)
    return x


def _forward(x_prompt, x_sample, p, rb=512):
    B1, L1, D = x_prompt.shape
    B2, L2, _ = x_sample.shape
    T1, T2 = B1 * L1, B2 * L2
    assert T1 % L2 == 0, "sample sequences must start on a multiple of their length"
    x = jnp.concatenate([x_prompt.reshape(T1, D), x_sample.reshape(T2, D)], axis=0)
    y = _encoder(x, [(0, B1, L1), (T1, B2, L2)], p, rb)
    return y[:T1].reshape(B1, L1, D), y[T1:].reshape(B2, L2, D)


def kernel(x_prompt, x_sample, hy_w_in, hy_b_in, hy_conv_w, hy_conv_b, hy_filt_w_in, hy_filt_b_in, hy_filt_w_mid, hy_filt_b_mid, hy_filt_freq, hy_filt_w_out, hy_filt_skip, hy_w_out, hy_b_out, da_w_qkv, da_lambda_q1, da_lambda_k1, da_lambda_q2, da_lambda_k2, da_subln_g, da_w_o, ln1_g, ln1_b, moe_router_w, moe_router_b, moe_w_gate, moe_b_gate, moe_w_up, moe_b_up, moe_w_down, moe_b_down, ln2_g, ln2_b):
    p = dict(hy_w_in=hy_w_in, hy_b_in=hy_b_in, hy_conv_w=hy_conv_w, hy_conv_b=hy_conv_b,
             hy_filt_w_in=hy_filt_w_in, hy_filt_b_in=hy_filt_b_in, hy_filt_w_mid=hy_filt_w_mid,
             hy_filt_b_mid=hy_filt_b_mid, hy_filt_freq=hy_filt_freq, hy_filt_w_out=hy_filt_w_out,
             hy_filt_skip=hy_filt_skip, hy_w_out=hy_w_out, hy_b_out=hy_b_out,
             da_w_qkv=da_w_qkv, da_lambda_q1=da_lambda_q1, da_lambda_k1=da_lambda_k1,
             da_lambda_q2=da_lambda_q2, da_lambda_k2=da_lambda_k2, da_subln_g=da_subln_g, da_w_o=da_w_o,
             ln1_g=ln1_g, ln1_b=ln1_b, moe_router_w=moe_router_w, moe_router_b=moe_router_b,
             moe_w_gate=moe_w_gate, moe_b_gate=moe_b_gate, moe_w_up=moe_w_up, moe_b_up=moe_b_up,
             moe_w_down=moe_w_down, moe_b_down=moe_b_down, ln2_g=ln2_g, ln2_b=ln2_b)
    return _forward(x_prompt, x_sample, p)
```

```python
import functools
import math

import jax
import jax.numpy as jnp
from jax import lax
from jax.experimental import pallas as pl
from jax.experimental.pallas import tpu as pltpu

DEPTH = 2
SHORT_CONV = 3
HYENA_ORDER = 2
FILTER_EMB = 33
FILTER_BANDS = (FILTER_EMB - 1) // 2
FILTER_INNER = 2
FAST_DECAY_PCT = 0.3
SLOW_DECAY_PCT = 1.5
DECAY_TARGET = 1e-2
MAX_DECAY = math.log(DECAY_TARGET) / FAST_DECAY_PCT
MIN_DECAY = math.log(DECAY_TARGET) / SLOW_DECAY_PCT
DIFF_HEAD_DIM = 128
ROPE_DIM = DIFF_HEAD_DIM // 4
ROPE_THETA = 500000.0
TOP_K = 4
SWIGLU_ALPHA = 1.702
SWIGLU_LIMIT = 7.0
LN_EPS = 1e-5
ALPHA = (2 * DEPTH) ** 0.25

LANES = 128
SUBLANES = 8
VMEM_LIMIT_BYTES = 60 * 1024 * 1024

HYENA_BLOCK = 512

F32 = jnp.float32
BF16 = jnp.bfloat16


def _params(semantics, vmem=VMEM_LIMIT_BYTES):
    return pltpu.CompilerParams(dimension_semantics=semantics, vmem_limit_bytes=vmem)


def _tile(n, pref):
    t = min(n, pref)
    assert n % t == 0, (n, t)
    return t


def _split_bf16(a):
    hi = a.astype(BF16)
    lo = (a - hi.astype(F32)).astype(BF16)
    return hi, lo


def _dot3(a, b):
    ah, al = _split_bf16(a)
    bh, bl = _split_bf16(b)
    d = functools.partial(jnp.dot, preferred_element_type=F32)
    return d(ah, bh) + d(al, bh) + d(ah, bl)


def _layer_norm(y, g, b):
    mu = jnp.mean(y, axis=-1, keepdims=True)
    yc = y - mu
    var = jnp.mean(yc * yc, axis=-1, keepdims=True)
    return yc * lax.rsqrt(var + LN_EPS) * g + b


def _proj_kernel(x_ref, w_ref, b_ref, o_ref):
    acc = jnp.dot(x_ref[...].astype(BF16), w_ref[...], preferred_element_type=F32)
    o_ref[...] = (acc + b_ref[...]).astype(o_ref.dtype)


def _project(x, w, b, out_dtype, tm=512, tn=2048):
    T, K = x.shape
    N = w.shape[1]
    tm, tn = _tile(T, tm), _tile(N, tn)
    return pl.pallas_call(
        _proj_kernel,
        grid=(N // tn, T // tm),
        in_specs=[pl.BlockSpec((tm, K), lambda n, m: (m, 0)),
                  pl.BlockSpec((K, tn), lambda n, m: (0, n)),
                  pl.BlockSpec((1, tn), lambda n, m: (0, n))],
        out_specs=pl.BlockSpec((tm, tn), lambda n, m: (m, n)),
        out_shape=jax.ShapeDtypeStruct((T, N), out_dtype),
        compiler_params=_params(("arbitrary", "arbitrary")),
        name="project",
    )(x, w, b.reshape(1, N))


def _qkv_kernel(x_ref, w_ref, invf_ref, o_ref, *, tm, split, l1, l2, n_rope_slabs):
    n = pl.program_id(0)
    m = pl.program_id(1)
    acc = jnp.dot(x_ref[...].astype(BF16), w_ref[...], preferred_element_type=F32)
    tn = acc.shape[1]

    @pl.when(n >= n_rope_slabs)
    def _():
        o_ref[...] = acc.astype(o_ref.dtype)

    @pl.when(n < n_rope_slabs)
    def _():
        start = m * tm
        pos0 = jnp.where(start < split, lax.rem(start, l1), lax.rem(jnp.maximum(start - split, 0), l2))
        row = lax.broadcasted_iota(jnp.int32, (tm, LANES), 0)
        lane = lax.broadcasted_iota(jnp.int32, (tm, LANES), 1)
        ang = (pos0 + row).astype(F32) * invf_ref[...]
        cos, sin = jnp.cos(ang), jnp.sin(ang)
        half = ROPE_DIM // 2
        c_tab = jnp.where(lane < ROPE_DIM, cos, 1.0)
        s_lo = jnp.where(lane < half, -sin, 0.0)
        s_hi = jnp.where((lane >= half) & (lane < ROPE_DIM), sin, 0.0)
        scale = jnp.where(n == 0, DIFF_HEAD_DIM ** -0.5 * math.log2(math.e), 1.0)
        for h in range(tn // LANES):
            xh = acc[:, h * LANES:(h + 1) * LANES]
            rot = (xh * c_tab + pltpu.roll(xh, LANES - half, 1) * s_lo
                   + pltpu.roll(xh, half, 1) * s_hi)
            o_ref[:, h * LANES:(h + 1) * LANES] = (rot * scale).astype(o_ref.dtype)


def _qkv_project(x, w, split, l1, l2, tm=512):
    T, K = x.shape
    N = w.shape[1]
    tn = N // 3
    tm = _tile(math.gcd(l1, l2), tm)
    fi = jnp.arange(LANES, dtype=jnp.int32) % (ROPE_DIM // 2)
    inv_freq = ROPE_THETA ** (-(2 * fi).astype(F32) / ROPE_DIM)
    inv_freq = jnp.where(jnp.arange(LANES) < ROPE_DIM, inv_freq, 0.0).reshape(1, LANES)
    return pl.pallas_call(
        functools.partial(_qkv_kernel, tm=tm, split=split, l1=l1, l2=l2, n_rope_slabs=2),
        grid=(3, T // tm),
        in_specs=[pl.BlockSpec((tm, K), lambda n, m: (m, 0)),
                  pl.BlockSpec((K, tn), lambda n, m: (0, n)),
                  pl.BlockSpec((1, LANES), lambda n, m: (0, 0))],
        out_specs=pl.BlockSpec((tm, tn), lambda n, m: (m, n)),
        out_shape=jax.ShapeDtypeStruct((T, N), BF16),
        compiler_params=_params(("arbitrary", "arbitrary")),
        name="qkv_project",
    )(x, w, inv_freq)


def _pack_bf16_pairs(y):
    half = y.shape[1] // 2
    r = pltpu.bitcast(y.astype(BF16).astype(F32), jnp.uint32)
    return (r[:, :half] & jnp.uint32(0xFFFF0000)) | (r[:, half:] >> 16)


def _unpack_bf16_pairs(w):
    hi = pltpu.bitcast(w & jnp.uint32(0xFFFF0000), F32)
    lo = pltpu.bitcast(w << 16, F32)
    return jnp.concatenate([hi, lo], axis=1).astype(BF16)


def _out_ln_kernel(a_ref, w_ref, b_ref, x_ref, g_ref, beta_ref, o_ref, p_ref):
    h = jnp.dot(a_ref[...], w_ref[...], preferred_element_type=F32) + b_ref[...]
    y = _layer_norm(ALPHA * x_ref[...] + h, g_ref[...], beta_ref[...])
    o_ref[...] = y
    p_ref[...] = _pack_bf16_pairs(y)


def _out_project_ln(a, w, b, x, g, beta, tm=512):
    T, K = a.shape
    D = w.shape[1]
    tm = _tile(T, tm)
    row = lambda i: (i, 0)
    fixed = lambda i: (0, 0)
    return pl.pallas_call(
        _out_ln_kernel,
        grid=(T // tm,),
        in_specs=[pl.BlockSpec((tm, K), row), pl.BlockSpec((K, D), fixed), pl.BlockSpec((1, D), fixed),
                  pl.BlockSpec((tm, D), row), pl.BlockSpec((1, D), fixed), pl.BlockSpec((1, D), fixed)],
        out_specs=[pl.BlockSpec((tm, D), row), pl.BlockSpec((tm, D // 2), row)],
        out_shape=[jax.ShapeDtypeStruct((T, D), F32), jax.ShapeDtypeStruct((T, D // 2), jnp.uint32)],
        compiler_params=_params(("arbitrary",)),
        name="out_project_ln",
    )(a, w, b.reshape(1, D), x, g.reshape(1, D), beta.reshape(1, D))


def _filter_kernel(z_ref, w_in_ref, b_in_ref, w_mid_ref, b_mid_ref, freq_ref, w_out_ref, delta_ref,
                   k_ref, nrm_ref, *, tl):
    r = pl.program_id(1)
    z = z_ref[...]
    h = jnp.sin(freq_ref[0:1, :] * (_dot3(z, w_in_ref[...]) + b_in_ref[...]))
    for j in range(FILTER_INNER):
        h = jnp.sin(freq_ref[j + 1:j + 2, :] * (_dot3(h, w_mid_ref[j]) + b_mid_ref[j:j + 1, :]))
    out = _dot3(h, w_out_ref[...]) * jnp.exp(-z[:, 0:1] * delta_ref[...])
    row = lax.broadcasted_iota(jnp.int32, out.shape, 0) + r * tl
    out = jnp.where(row == 0, 0.0, out)
    k_ref[0] = out

    @pl.when(r == 0)
    def _():
        nrm_ref[...] = jnp.zeros_like(nrm_ref)

    nrm_ref[0] = nrm_ref[0] + jnp.sum(jnp.abs(out), axis=0, keepdims=True)


def _hyena_filters(L, D, w_in, b_in, w_mid, b_mid, freq, w_out, tl=256):
    t = jnp.linspace(0.0, 1.0, L, dtype=F32)[:, None]
    omega = (2.0 * math.pi / L) * jnp.arange(L, dtype=F32)
    bands = jnp.linspace(1e-4, FILTER_BANDS - 1, FILTER_BANDS, dtype=F32)
    phase = omega[:, None] * bands[None, :]
    z = jnp.concatenate([t, jnp.cos(phase), -jnp.sin(phase)], axis=-1)
    z_lag = jnp.concatenate([z[:1], z[1:][::-1], z], axis=0)
    deltas = jnp.abs(jnp.linspace(MIN_DECAY, MAX_DECAY, D, dtype=F32)).reshape(1, D)
    tl = _tile(L, tl)
    n_neg = L // tl
    hid = w_in.shape[1]
    fixed2 = lambda n, r: (0, 0)
    return pl.pallas_call(
        functools.partial(_filter_kernel, tl=tl),
        grid=(HYENA_ORDER, 2 * L // tl),
        in_specs=[pl.BlockSpec((tl, FILTER_EMB), lambda n, r: (r, 0)),
                  pl.BlockSpec((FILTER_EMB, hid), fixed2), pl.BlockSpec((1, hid), fixed2),
                  pl.BlockSpec((FILTER_INNER, hid, hid), lambda n, r: (0, 0, 0)),
                  pl.BlockSpec((FILTER_INNER, hid), fixed2),
                  pl.BlockSpec((FILTER_INNER + 1, hid), fixed2),
                  pl.BlockSpec((hid, D), lambda n, r: (0, 2 * n + (r < n_neg).astype(jnp.int32))),
                  pl.BlockSpec((1, D), fixed2)],
        out_specs=[pl.BlockSpec((1, tl, D), lambda n, r: (n, r, 0)),
                   pl.BlockSpec((1, SUBLANES, D), lambda n, r: (n, 0, 0))],
        out_shape=[jax.ShapeDtypeStruct((HYENA_ORDER, 2 * L, D), F32),
                   jax.ShapeDtypeStruct((HYENA_ORDER, SUBLANES, D), F32)],
        compiler_params=_params(("arbitrary", "arbitrary")),
        name="hyena_filter",
    )(z_lag, w_in, b_in.reshape(1, hid), w_mid, b_mid, freq, w_out, deltas)


def _dft_matrices(bk):
    nfp = (bk + 1 + SUBLANES - 1) // SUBLANES * SUBLANES
    f = jnp.arange(nfp, dtype=jnp.int32)[:, None]
    r = jnp.arange(bk, dtype=jnp.int32)[None, :]
    ang = (2.0 * math.pi / (2 * bk)) * ((f * r) % (2 * bk)).astype(F32)
    live = f <= bk
    cos = jnp.where(live, jnp.cos(ang), 0.0)
    sin = jnp.where(live, jnp.sin(ang), 0.0)
    fwd = jnp.concatenate([cos, -sin], axis=0)
    wgt = jnp.where((f == 0) | (f == bk), 1.0, 2.0) / (2 * bk)
    inv = jnp.concatenate([cos * wgt, -sin * wgt], axis=0).T
    return fwd.astype(BF16), inv.astype(BF16), nfp


def _spectra_kernel(k_ref, nrm_ref, f_ref, c_ref, a_prev, k_prev, *, nfp):
    s = pl.program_id(2)
    kb = k_ref[0]
    a = jnp.dot(f_ref[...], kb.astype(BF16), preferred_element_type=F32)

    @pl.when(s > 0)
    def _():
        row = lax.broadcasted_iota(jnp.int32, a.shape, 0)
        sign = jnp.where((row & 1) == 0, 1.0, -1.0)
        k0 = jnp.where(row < nfp, k_prev[0:1, :], 0.0)
        c_ref[0, 0] = (a + sign * (a_prev[...] - k0)) / nrm_ref[0, 0:1, :]

    a_prev[...] = a
    k_prev[...] = kb[0:SUBLANES, :]


def _hyena_spectra(kappa, nrm, fwd, bk, nfp, dt=256):
    order, two_l, D = kappa.shape
    n_blk = two_l // bk
    dt = _tile(D, dt)
    return pl.pallas_call(
        functools.partial(_spectra_kernel, nfp=nfp),
        grid=(order, D // dt, n_blk),
        in_specs=[pl.BlockSpec((1, bk, dt), lambda n, d, s: (n, s, d)),
                  pl.BlockSpec((1, SUBLANES, dt), lambda n, d, s: (n, 0, d)),
                  pl.BlockSpec((2 * nfp, bk), lambda n, d, s: (0, 0))],
        out_specs=pl.BlockSpec((1, 1, 2 * nfp, dt), lambda n, d, s: (n, jnp.maximum(s - 1, 0), 0, d)),
        out_shape=jax.ShapeDtypeStruct((order, n_blk - 1, 2 * nfp, D), F32),
        scratch_shapes=[pltpu.VMEM((2 * nfp, dt), F32), pltpu.VMEM((SUBLANES, dt), F32)],
        compiler_params=_params(("arbitrary", "arbitrary", "arbitrary")),
        name="hyena_spectra",
    )(kappa, nrm, fwd)


def _short_conv_block(src_ref, w_ref, b_ref, j, nb, bk):
    cur = src_ref[0, j * bk:(j + 1) * bk, :]
    row = lax.broadcasted_iota(jnp.int32, cur.shape, 0)
    if j > 0:
        before = src_ref[0, j * bk - SUBLANES:j * bk, :][SUBLANES - 1:SUBLANES, :]
    else:
        before = jnp.zeros_like(cur[0:1, :])
    if j < nb - 1:
        after = src_ref[0, (j + 1) * bk:(j + 1) * bk + SUBLANES, :][0:1, :]
    else:
        after = jnp.zeros_like(cur[0:1, :])
    up = jnp.where(row == 0, before, pltpu.roll(cur, 1, 0))
    dn = jnp.where(row == bk - 1, after, pltpu.roll(cur, bk - 1, 0))
    return b_ref[...] + up * w_ref[0:1, :] + cur * w_ref[1:2, :] + dn * w_ref[2:3, :]


def _long_conv_kernel(*refs, nb, bk, nfp, conv_input):
    if conv_input:
        (z_ref, zw_ref, zb_ref, g_ref, gw_ref, gb_ref, skip_ref, c_ref, fwd_ref, inv_ref,
         o_ref, zf_ref, y_ref) = refs
    else:
        (z_ref, g_ref, gw_ref, gb_ref, skip_ref, c_ref, fwd_ref, inv_ref, o_ref, zf_ref, y_ref) = refs

    def z_block(j):
        if conv_input:
            return _short_conv_block(z_ref, zw_ref, zb_ref, j, nb, bk)
        return z_ref[0, j * bk:(j + 1) * bk, :]

    for j in range(nb):
        zf_ref[j] = jnp.dot(fwd_ref[...], z_block(j).astype(BF16), preferred_element_type=F32)

    for i in range(nb):
        def chunk(c, carry, i=i):
            r0 = pl.multiple_of(c * SUBLANES, SUBLANES)
            re_rows, im_rows = pl.ds(r0, SUBLANES), pl.ds(nfp + r0, SUBLANES)
            acc_re = acc_im = None
            for j in range(nb):
                m = i - j + nb - 1
                c_re, c_im = c_ref[0, m, re_rows, :], c_ref[0, m, im_rows, :]
                z_re, z_im = zf_ref[j, re_rows, :], zf_ref[j, im_rows, :]
                t_re = c_re * z_re - c_im * z_im
                t_im = c_re * z_im + c_im * z_re
                acc_re = t_re if acc_re is None else acc_re + t_re
                acc_im = t_im if acc_im is None else acc_im + t_im
            y_ref[re_rows, :] = acc_re
            y_ref[im_rows, :] = acc_im
            return carry

        lax.fori_loop(0, nfp // SUBLANES, chunk, 0, unroll=max(1, 8 // nb))
        y = jnp.dot(inv_ref[...], y_ref[...].astype(BF16), preferred_element_type=F32)
        gate = _short_conv_block(g_ref, gw_ref, gb_ref, i, nb, bk)
        o_ref[0, i * bk:(i + 1) * bk, :] = (gate * (y + skip_ref[...] * z_block(i))).astype(o_ref.dtype)


def _long_conv(u3, z3, order, spectra, fwd, inv, conv_w, conv_b, skip, *, b0, nbatch, bk, nfp, dt, out_dtype):
    _, L, three_d = u3.shape
    D = three_d // 3
    nb = L // bk
    n_d = D // dt
    conv_input = z3 is None
    slab = lambda k: (lambda d, b: (b0 + b, 0, k * n_d + d))
    cols = lambda k: (lambda d, b: (0, k * n_d + d))
    seq_block = (1, L, dt)
    own = lambda d, b: (b, 0, d)
    in_specs, args = [], []
    if conv_input:
        in_specs += [pl.BlockSpec(seq_block, slab(0)), pl.BlockSpec((SHORT_CONV, dt), cols(0)),
                     pl.BlockSpec((1, dt), cols(0))]
        args += [u3, conv_w, conv_b]
    else:
        in_specs += [pl.BlockSpec(seq_block, own)]
        args += [z3]
    in_specs += [pl.BlockSpec(seq_block, slab(order + 1)), pl.BlockSpec((SHORT_CONV, dt), cols(order + 1)),
                 pl.BlockSpec((1, dt), cols(order + 1)),
                 pl.BlockSpec((1, dt), lambda d, b: (0, d)),
                 pl.BlockSpec((1, 2 * nb - 1, 2 * nfp, dt), lambda d, b: (order, 0, 0, d),
                              pipeline_mode=pl.Buffered(1)),
                 pl.BlockSpec((2 * nfp, bk), lambda d, b: (0, 0)),
                 pl.BlockSpec((bk, 2 * nfp), lambda d, b: (0, 0))]
    args += [u3, conv_w, conv_b, skip[order:order + 1], spectra, fwd, inv]
    return pl.pallas_call(
        functools.partial(_long_conv_kernel, nb=nb, bk=bk, nfp=nfp, conv_input=conv_input),
        grid=(n_d, nbatch),
        in_specs=in_specs,
        out_specs=pl.BlockSpec(seq_block, own),
        out_shape=jax.ShapeDtypeStruct((nbatch, L, D), out_dtype),
        scratch_shapes=[pltpu.VMEM((nb, 2 * nfp, dt), F32), pltpu.VMEM((2 * nfp, dt), F32)],
        compiler_params=_params(("arbitrary", "arbitrary")),
        name="hyena_long_conv",
    )(*args)


def _hyena_mixer(x, groups, p):
    T, D = x.shape
    u = _project(x, p["hy_w_in"].astype(BF16), p["hy_b_in"], F32)
    conv_w, conv_b, skip = p["hy_conv_w"], p["hy_conv_b"].reshape(1, 3 * D), p["hy_filt_skip"]
    outs = []
    for first, nbatch, L in groups:
        bk = min(L, HYENA_BLOCK)
        dt = _tile(D, 256 if L <= 2048 else 128)
        fwd, inv, nfp = _dft_matrices(bk)
        kappa, nrm = _hyena_filters(L, D, p["hy_filt_w_in"], p["hy_filt_b_in"], p["hy_filt_w_mid"],
                                    p["hy_filt_b_mid"], p["hy_filt_freq"], p["hy_filt_w_out"])
        spectra = _hyena_spectra(kappa, nrm, fwd, bk, nfp)
        kw = dict(b0=first // L, nbatch=nbatch, bk=bk, nfp=nfp, dt=dt)
        u3 = u.reshape(T // L, L, 3 * D)
        z1 = _long_conv(u3, None, 0, spectra, fwd, inv, conv_w, conv_b, skip, out_dtype=F32, **kw)
        z2 = _long_conv(u3, z1, 1, spectra, fwd, inv, conv_w, conv_b, skip, out_dtype=BF16, **kw)
        outs.append(z2.reshape(nbatch * L, D))
    return jnp.concatenate(outs, axis=0)


def _attn_kernel(q_ref, k_ref, v_ref, lam_ref, g_ref, o_ref, *, lam_init):
    dh = DIFF_HEAD_DIM
    lam_dots = jnp.sum(lam_ref[0:2, :] * lam_ref[2:4, :], axis=-1, keepdims=True)
    lam = jnp.exp(lam_dots[0:1, :]) - jnp.exp(lam_dots[1:2, :]) + lam_init
    q, k, v = q_ref[...], k_ref[...], v_ref[...]
    parts = []
    for c in range(2):
        s = lax.dot_general(q[:, c * dh:(c + 1) * dh], k[:, c * dh:(c + 1) * dh],
                            (((1,), (1,)), ((), ())), preferred_element_type=F32)
        e = jnp.exp2(s - jnp.max(s, axis=-1, keepdims=True))
        parts.append((jnp.dot(e.astype(BF16), v, preferred_element_type=F32),
                      jnp.sum(e, axis=-1, keepdims=True)))
    (o0, l0), (o1, l1) = parts
    o = o0 * (1.0 / l0) - o1 * (lam / l1)
    o = o * lax.rsqrt(jnp.mean(o * o, axis=-1, keepdims=True) + LN_EPS) * g_ref[...]
    o_ref[...] = (o * (1.0 - lam_init)).astype(o_ref.dtype)


def _diff_attention(qkv, lam_vecs, subln_g, layer_idx, *, first, nbatch, L, tq):
    T, three_d = qkv.shape
    D = three_d // 3
    hw = 2 * DIFF_HEAD_DIM
    H = D // hw
    tq = _tile(L, tq)
    nq = L // tq
    q0, b0 = first // tq, first // L
    lam_init = 0.8 - 0.6 * math.exp(-0.3 * layer_idx)
    return pl.pallas_call(
        functools.partial(_attn_kernel, lam_init=lam_init),
        grid=(nbatch, H, nq),
        in_specs=[pl.BlockSpec((tq, hw), lambda b, h, i: (q0 + b * nq + i, h)),
                  pl.BlockSpec((L, hw), lambda b, h, i: (b0 + b, H + h)),
                  pl.BlockSpec((L, hw), lambda b, h, i: (b0 + b, 2 * H + h)),
                  pl.BlockSpec((4, DIFF_HEAD_DIM), lambda b, h, i: (0, 0)),
                  pl.BlockSpec((1, hw), lambda b, h, i: (0, 0))],
        out_specs=pl.BlockSpec((tq, hw), lambda b, h, i: (b * nq + i, h)),
        out_shape=jax.ShapeDtypeStruct((nbatch * L, D), BF16),
        compiler_params=_params(("arbitrary", "arbitrary", "arbitrary")),
        name="diff_attention",
    )(qkv, qkv, qkv, lam_vecs, subln_g.reshape(1, hw))


def _attention_mixer(x, groups, p, layer_idx):
    split = groups[1][0] if len(groups) > 1 else x.shape[0]
    l1, l2 = groups[0][2], groups[-1][2]
    qkv = _qkv_project(x, p["da_w_qkv"].astype(BF16), split, l1, l2)
    lam_vecs = jnp.stack([p["da_lambda_q1"], p["da_lambda_q2"], p["da_lambda_k1"], p["da_lambda_k2"]])
    outs = [_diff_attention(qkv, lam_vecs, p["da_subln_g"], layer_idx, first=first, nbatch=nbatch, L=L,
                            tq=1024 if L <= 2048 else 256)
            for first, nbatch, L in groups]
    return jnp.concatenate(outs, axis=0)


def _router_kernel(x_ref, w_ref, b_ref, idx_ref, gate_ref, rank_ref, cnt_ref, base_ref, *, n_experts):
    i = pl.program_id(0)

    @pl.when(i == 0)
    def _():
        base_ref[...] = jnp.zeros_like(base_ref)

    logits = _dot3(x_ref[...], w_ref[...]) + b_ref[...]
    tm = logits.shape[0]
    lane = lax.broadcasted_iota(jnp.int32, (tm, LANES), 1)
    neg_inf = jnp.float32(-jnp.inf)
    lane_f = lane.astype(F32)
    live = jnp.where(lane < n_experts, logits, neg_inf)
    vals, picks, hots = [], [], []
    for _ in range(TOP_K):
        best = jnp.max(live, axis=-1, keepdims=True)
        pick = jnp.min(jnp.where(live == best, lane_f, float(LANES)), axis=-1, keepdims=True)
        hot = lane_f == pick
        live = jnp.where(hot, neg_inf, live)
        vals.append(best)
        picks.append(pick.astype(jnp.int32))
        hots.append(hot)
    exps = [jnp.exp(v - vals[0]) for v in vals]
    denom = exps[0] + exps[1] + exps[2] + exps[3]
    chosen = jnp.zeros((tm, LANES), F32)
    for hot in hots:
        chosen = chosen + hot.astype(F32)
    r_i = lax.broadcasted_iota(jnp.int32, (tm, tm), 0)
    c_i = lax.broadcasted_iota(jnp.int32, (tm, tm), 1)
    earlier = jnp.where(r_i > c_i, 1.0, 0.0).astype(BF16)
    before = base_ref[0:1, :] + jnp.dot(earlier, chosen.astype(BF16), preferred_element_type=F32)
    idx_out = jnp.zeros((tm, LANES), jnp.int32)
    rank_out = jnp.zeros((tm, LANES), jnp.int32)
    gate_out = jnp.zeros((tm, LANES), F32)
    for k in range(TOP_K):
        rank_k = jnp.sum(jnp.where(hots[k], before, 0.0), axis=-1, keepdims=True).astype(jnp.int32)
        idx_out = jnp.where(lane == k, picks[k], idx_out)
        rank_out = jnp.where(lane == k, rank_k, rank_out)
        gate_out = jnp.where(lane == k, exps[k] / denom, gate_out)
    idx_ref[...] = idx_out
    rank_ref[...] = rank_out
    gate_ref[...] = gate_out
    base_ref[...] = base_ref[...] + jnp.sum(chosen, axis=0, keepdims=True)
    cnt_ref[...] = base_ref[...]


def _route(x, router_w, router_b, tm=512):
    T, D = x.shape
    E = router_w.shape[1]
    tm = _tile(T, tm)
    w = jnp.pad(router_w, ((0, 0), (0, LANES - E)))
    b = jnp.pad(router_b, (0, LANES - E)).reshape(1, LANES)
    row = lambda i: (i, 0)
    fixed = lambda i: (0, 0)
    wide = pl.BlockSpec((tm, LANES), row)
    return pl.pallas_call(
        functools.partial(_router_kernel, n_experts=E),
        grid=(T // tm,),
        in_specs=[pl.BlockSpec((tm, D), row), pl.BlockSpec((D, LANES), fixed), pl.BlockSpec((1, LANES), fixed)],
        out_specs=[wide, wide, wide, pl.BlockSpec((SUBLANES, LANES), fixed)],
        out_shape=[jax.ShapeDtypeStruct((T, LANES), jnp.int32), jax.ShapeDtypeStruct((T, LANES), F32),
                   jax.ShapeDtypeStruct((T, LANES), jnp.int32), jax.ShapeDtypeStruct((SUBLANES, LANES), F32)],
        scratch_shapes=[pltpu.VMEM((SUBLANES, LANES), F32)],
        compiler_params=_params(("arbitrary",)),
        name="moe_route",
    )(x, w, b)


def _row_copy(src, src_row, dst, dst_row, sem):
    return pltpu.make_async_copy(src.at[pl.ds(src_row, 1)], dst.at[pl.ds(dst_row, 1)], sem)


def _dispatch_kernel(dest_ref, x_ref, xs_in, xs_out, sem, *, tm):
    del xs_in

    def issue(t, carry):
        for k in range(TOP_K):
            _row_copy(x_ref, t, xs_out, dest_ref[t * TOP_K + k], sem).start(priority=k % 2)
        return carry

    def drain(t, carry):
        for k in range(TOP_K):
            _row_copy(x_ref, 0, xs_out, 0, sem).wait()
        return carry

    lax.fori_loop(0, tm, issue, 0)
    lax.fori_loop(0, tm, drain, 0)


def _dispatch(x_packed, dest_flat, n_rows, tm=256):
    T, half = x_packed.shape
    tm = _tile(T, tm)
    return pl.pallas_call(
        functools.partial(_dispatch_kernel, tm=tm),
        grid=(T // tm,),
        in_specs=[pl.BlockSpec((tm * TOP_K,), lambda i: (i,), memory_space=pltpu.SMEM),
                  pl.BlockSpec((tm, half), lambda i: (i, 0)), pl.BlockSpec(memory_space=pl.ANY)],
        out_specs=pl.BlockSpec(memory_space=pl.ANY),
        out_shape=jax.ShapeDtypeStruct((n_rows, half), jnp.uint32),
        scratch_shapes=[pltpu.SemaphoreType.DMA(())],
        input_output_aliases={2: 0},
        compiler_params=_params(("arbitrary",)),
        name="moe_dispatch",
    )(dest_flat, x_packed, jnp.zeros((n_rows, half), jnp.uint32))


def _expert_kernel(tile_expert_ref, n_used_ref, x_ref, wg_ref, bg_ref, wu_ref, bu_ref, wd_ref, bd_ref, o_ref):
    del tile_expert_ref

    @pl.when(pl.program_id(0) < n_used_ref[0])
    def _():
        x = _unpack_bf16_pairs(x_ref[...])
        gate = jnp.dot(x, wg_ref[0], preferred_element_type=F32) + bg_ref[0]
        up = jnp.dot(x, wu_ref[0], preferred_element_type=F32) + bu_ref[0]
        gate = jnp.minimum(gate, SWIGLU_LIMIT)
        up = jnp.clip(up, -SWIGLU_LIMIT, SWIGLU_LIMIT)
        h = gate * (1.0 / (1.0 + jnp.exp(-SWIGLU_ALPHA * gate))) * (up + 1.0)
        o_ref[...] = jnp.dot(h.astype(BF16), wd_ref[0], preferred_element_type=F32) + bd_ref[0]


def _experts(xs, tile_expert, n_used, wg, bg, wu, bu, wd, bd, rb):
    P, half = xs.shape
    E, D, F = wg.shape
    n_tiles = P // rb
    of_expert = lambda i, te, nu: (te[i], 0, 0)
    resident = dict(pipeline_mode=pl.Buffered(1))
    return pl.pallas_call(
        _expert_kernel,
        grid_spec=pltpu.PrefetchScalarGridSpec(
            num_scalar_prefetch=2,
            grid=(n_tiles,),
            in_specs=[pl.BlockSpec((rb, half), lambda i, te, nu: (i, 0)),
                      pl.BlockSpec((1, D, F), of_expert, **resident), pl.BlockSpec((1, 1, F), of_expert),
                      pl.BlockSpec((1, D, F), of_expert, **resident), pl.BlockSpec((1, 1, F), of_expert),
                      pl.BlockSpec((1, F, D), of_expert, **resident), pl.BlockSpec((1, 1, D), of_expert)],
            out_specs=pl.BlockSpec((rb, D), lambda i, te, nu: (i, 0))),
        out_shape=jax.ShapeDtypeStruct((P, D), F32),
        compiler_params=_params(("arbitrary",)),
        name="moe_experts",
    )(tile_expert, n_used, xs, wg, bg.reshape(E, 1, F), wu, bu.reshape(E, 1, F), wd, bd.reshape(E, 1, D))


def _combine_kernel(dest_ref, dest_next_ref, gate_ref, x_ref, g_ref, beta_ref, ys_hbm, *rest,
                    tc, n_steps, n_first):
    *o_refs, rows, sem = rest
    i = pl.program_id(0)
    slot = lax.rem(i, 2)

    def gather(d_ref, s):
        def issue(t, carry):
            for k in range(TOP_K):
                _row_copy(ys_hbm, d_ref[t * TOP_K + k], rows.at[s, k], t, sem.at[s]).start(priority=k % 2)
            return carry
        lax.fori_loop(0, tc, issue, 0)

    @pl.when(i == 0)
    def _():
        gather(dest_ref, 0)

    @pl.when(i + 1 < n_steps)
    def _():
        gather(dest_next_ref, 1 - slot)

    def drain(t, carry):
        for k in range(TOP_K):
            _row_copy(ys_hbm, 0, rows.at[slot, k], 0, sem.at[slot]).wait()
        return carry

    lax.fori_loop(0, tc, drain, 0)
    gates = gate_ref[...]
    mix = gates[:, 0:1] * rows[slot, 0]
    for k in range(1, TOP_K):
        mix = mix + gates[:, k:k + 1] * rows[slot, k]
    y = _layer_norm(ALPHA * x_ref[...] + mix, g_ref[...], beta_ref[...])
    if n_first is None:
        o_refs[0][...] = y
    else:
        @pl.when(i < n_first)
        def _():
            o_refs[0][...] = y

        @pl.when(i >= n_first)
        def _():
            o_refs[1][...] = y


def _combine_ln(ys, dest_flat, gates, x, g, beta, split=None, tc=128):
    T, D = x.shape
    tc = _tile(T if split is None else math.gcd(split, T - split), tc)
    n_steps = T // tc
    row = lambda i: (i, 0)
    fixed = lambda i: (0, 0)
    tile = (tc, D)
    if split is None:
        n_first = None
        out_specs = pl.BlockSpec(tile, row)
        out_shape = jax.ShapeDtypeStruct((T, D), F32)
    else:
        n_first = split // tc
        out_specs = [pl.BlockSpec(tile, lambda i: (jnp.minimum(i, n_first - 1), 0)),
                     pl.BlockSpec(tile, lambda i: (jnp.maximum(i - n_first, 0), 0))]
        out_shape = [jax.ShapeDtypeStruct((split, D), F32), jax.ShapeDtypeStruct((T - split, D), F32)]
    return pl.pallas_call(
        functools.partial(_combine_kernel, tc=tc, n_steps=n_steps, n_first=n_first),
        grid=(n_steps,),
        in_specs=[pl.BlockSpec((tc * TOP_K,), lambda i: (i,), memory_space=pltpu.SMEM),
                  pl.BlockSpec((tc * TOP_K,), lambda i: (jnp.minimum(i + 1, n_steps - 1),),
                               memory_space=pltpu.SMEM),
                  pl.BlockSpec((tc, LANES), row), pl.BlockSpec(tile, row),
                  pl.BlockSpec((1, D), fixed), pl.BlockSpec((1, D), fixed),
                  pl.BlockSpec(memory_space=pl.ANY)],
        out_specs=out_specs,
        out_shape=out_shape,
        scratch_shapes=[pltpu.VMEM((2, TOP_K, tc, D), F32), pltpu.SemaphoreType.DMA((2,))],
        compiler_params=_params(("arbitrary",)),
        name="moe_combine_ln",
    )(dest_flat, dest_flat, gates, x, g.reshape(1, D), beta.reshape(1, D), ys)


def _moe_ln(x, x_packed, p, i, rb, split=None):
    T, D = x.shape
    E = p["moe_router_w"].shape[-1]
    idx, gates, rank, counts = _route(x, p["moe_router_w"][i], p["moe_router_b"][i])
    counts = counts[0, :E].astype(jnp.int32)
    padded = (counts + rb - 1) // rb * rb
    padded_end = jnp.cumsum(padded)
    padded_start = padded_end - padded
    n_tiles = (T * TOP_K) // rb + E
    tile_expert = jnp.minimum(jnp.searchsorted(padded_end, jnp.arange(n_tiles, dtype=jnp.int32) * rb, side="right"),
                              E - 1).astype(jnp.int32)
    n_used = (padded_end[-1:] // rb).astype(jnp.int32)
    dest = (padded_start[idx[:, :TOP_K]] + rank[:, :TOP_K]).reshape(T * TOP_K)
    xs = _dispatch(x_packed, dest, n_tiles * rb)
    ys = _experts(xs, tile_expert, n_used, p["moe_w_gate"][i].astype(BF16), p["moe_b_gate"][i],
                  p["moe_w_up"][i].astype(BF16), p["moe_b_up"][i], p["moe_w_down"][i].astype(BF16),
                  p["moe_b_down"][i], rb)
    return _combine_ln(ys, dest, gates, x, p["ln2_g"][i], p["ln2_b"][i], split)


def _encoder(x, groups, p, rb):
    for i in range(DEPTH):
        j = i // 2
        if i % 2 == 0:
            pj = {k: v[j] for k, v in p.items() if k.startswith("hy_")}
            a = _hyena_mixer(x, groups, pj)
            w_o, b_o = pj["hy_w_out"], pj["hy_b_out"]
        else:
            pj = {k: v[j] for k, v in p.items() if k.startswith("da_")}
            a = _attention_mixer(x, groups, pj, i)
            w_o, b_o = pj["da_w_o"], jnp.zeros((x.shape[1],), F32)
        x, x_packed = _out_project_ln(a, w_o.astype(BF16), b_o, x, p["ln1_g"][i], p["ln1_b"][i])
        last = i == DEPTH - 1
        x = _moe_ln(x, x_packed, p, i, rb, groups[1][0] if last and len(groups) > 1 else None)
    return x


def _forward(x_prompt, x_sample, p, rb=512):
    B1, L1, D = x_prompt.shape
    B2, L2, _ = x_sample.shape
    T1, T2 = B1 * L1, B2 * L2
    assert T1 % L2 == 0, "sample sequences must start on a multiple of their length"
    x = jnp.concatenate([x_prompt.reshape(T1, D), x_sample.reshape(T2, D)], axis=0)
    y1, y2 = _encoder(x, [(0, B1, L1), (T1, B2, L2)], p, rb)
    return y1.reshape(B1, L1, D), y2.reshape(B2, L2, D)


def kernel(x_prompt, x_sample, hy_w_in, hy_b_in, hy_conv_w, hy_conv_b, hy_filt_w_in, hy_filt_b_in, hy_filt_w_mid, hy_filt_b_mid, hy_filt_freq, hy_filt_w_out, hy_filt_skip, hy_w_out, hy_b_out, da_w_qkv, da_lambda_q1, da_lambda_k1, da_lambda_q2, da_lambda_k2, da_subln_g, da_w_o, ln1_g, ln1_b, moe_router_w, moe_router_b, moe_w_gate, moe_b_gate, moe_w_up, moe_b_up, moe_w_down, moe_b_down, ln2_g, ln2_b):
    p = dict(hy_w_in=hy_w_in, hy_b_in=hy_b_in, hy_conv_w=hy_conv_w, hy_conv_b=hy_conv_b,
             hy_filt_w_in=hy_filt_w_in, hy_filt_b_in=hy_filt_b_in, hy_filt_w_mid=hy_filt_w_mid,
             hy_filt_b_mid=hy_filt_b_mid, hy_filt_freq=hy_filt_freq, hy_filt_w_out=hy_filt_w_out,
             hy_filt_skip=hy_filt_skip, hy_w_out=hy_w_out, hy_b_out=hy_b_out,
             da_w_qkv=da_w_qkv, da_lambda_q1=da_lambda_q1, da_lambda_k1=da_lambda_k1,
             da_lambda_q2=da_lambda_q2, da_lambda_k2=da_lambda_k2, da_subln_g=da_subln_g, da_w_o=da_w_o,
             ln1_g=ln1_g, ln1_b=ln1_b, moe_router_w=moe_router_w, moe_router_b=moe_router_b,
             moe_w_gate=moe_w_gate, moe_b_gate=moe_b_gate, moe_w_up=moe_w_up, moe_b_up=moe_b_up,
             moe_w_down=moe_w_down, moe_b_down=moe_b_down, ln2_g=ln2_g, ln2_b=ln2_b)
    return _forward(x_prompt, x_sample, p)
```

```python
import functools
import math

import jax
import jax.numpy as jnp
from jax import lax
from jax.experimental import pallas as pl
from jax.experimental.pallas import tpu as pltpu

DEPTH = 2
SHORT_CONV = 3
HYENA_ORDER = 2
FILTER_EMB = 33
FILTER_BANDS = (FILTER_EMB - 1) // 2
FILTER_INNER = 2
FAST_DECAY_PCT = 0.3
SLOW_DECAY_PCT = 1.5
DECAY_TARGET = 1e-2
MAX_DECAY = math.log(DECAY_TARGET) / FAST_DECAY_PCT
MIN_DECAY = math.log(DECAY_TARGET) / SLOW_DECAY_PCT
DIFF_HEAD_DIM = 128
ROPE_DIM = DIFF_HEAD_DIM // 4
ROPE_THETA = 500000.0
TOP_K = 4
SWIGLU_ALPHA = 1.702
SWIGLU_LIMIT = 7.0
LN_EPS = 1e-5
ALPHA = (2 * DEPTH) ** 0.25

LANES = 128
SUBLANES = 8
VMEM_LIMIT_BYTES = 60 * 1024 * 1024

HYENA_BLOCK = 512

ATTN_Q_TILE = 512
ATTN_KEY_CHUNK = 2048

F32 = jnp.float32
BF16 = jnp.bfloat16


def _params(semantics, vmem=VMEM_LIMIT_BYTES):
    return pltpu.CompilerParams(dimension_semantics=semantics, vmem_limit_bytes=vmem)


def _tile(n, pref):
    t = min(n, pref)
    assert n % t == 0, (n, t)
    return t


def _split_bf16(a):
    hi = a.astype(BF16)
    lo = (a - hi.astype(F32)).astype(BF16)
    return hi, lo


def _dot3(a, b):
    ah, al = _split_bf16(a)
    bh, bl = _split_bf16(b)
    d = functools.partial(jnp.dot, preferred_element_type=F32)
    return d(ah, bh) + d(al, bh) + d(ah, bl)


def _layer_norm(y, g, b):
    mu = jnp.mean(y, axis=-1, keepdims=True)
    yc = y - mu
    var = jnp.mean(yc * yc, axis=-1, keepdims=True)
    return yc * lax.rsqrt(var + LN_EPS) * g + b


def _proj_kernel(x_ref, w_ref, b_ref, o_ref):
    acc = jnp.dot(x_ref[...].astype(BF16), w_ref[...], preferred_element_type=F32)
    o_ref[...] = (acc + b_ref[...]).astype(o_ref.dtype)


def _project(x, w, b, out_dtype, tm=512, tn=2048):
    T, K = x.shape
    N = w.shape[1]
    tm, tn = _tile(T, tm), _tile(N, tn)
    return pl.pallas_call(
        _proj_kernel,
        grid=(N // tn, T // tm),
        in_specs=[pl.BlockSpec((tm, K), lambda n, m: (m, 0)),
                  pl.BlockSpec((K, tn), lambda n, m: (0, n)),
                  pl.BlockSpec((1, tn), lambda n, m: (0, n))],
        out_specs=pl.BlockSpec((tm, tn), lambda n, m: (m, n)),
        out_shape=jax.ShapeDtypeStruct((T, N), out_dtype),
        compiler_params=_params(("arbitrary", "arbitrary")),
        name="project",
    )(x, w, b.reshape(1, N))


def _qkv_kernel(x_ref, w_ref, rope_ref, o_ref, *, n_rope_slabs):
    n = pl.program_id(0)
    acc = jnp.dot(x_ref[...].astype(BF16), w_ref[...], preferred_element_type=F32)
    tn = acc.shape[1]

    @pl.when(n >= n_rope_slabs)
    def _():
        o_ref[...] = acc.astype(o_ref.dtype)

    @pl.when(n < n_rope_slabs)
    def _():
        scale = jnp.where(n == 0, DIFF_HEAD_DIM ** -0.5 * math.log2(math.e), 1.0)
        c_tab, s_lo, s_hi = rope_ref[0] * scale, rope_ref[1] * scale, rope_ref[2] * scale
        half = ROPE_DIM // 2
        for h in range(tn // LANES):
            xh = acc[:, h * LANES:(h + 1) * LANES]
            rot = (xh * c_tab + pltpu.roll(xh, LANES - half, 1) * s_lo
                   + pltpu.roll(xh, half, 1) * s_hi)
            o_ref[:, h * LANES:(h + 1) * LANES] = rot.astype(o_ref.dtype)


def _rope_tables(n_pos):
    half = ROPE_DIM // 2
    inv_freq = ROPE_THETA ** (-jnp.arange(0, ROPE_DIM, 2, dtype=F32) / ROPE_DIM)
    ang = jnp.arange(n_pos, dtype=F32)[:, None] * inv_freq[None, :]
    cos, sin = jnp.cos(ang), jnp.sin(ang)
    rest = LANES - ROPE_DIM
    pad = lambda parts, fill: jnp.concatenate(parts + [jnp.full((n_pos, rest), fill, F32)], axis=1)
    zero = jnp.zeros_like(sin)
    return jnp.stack([pad([cos, cos], 1.0), pad([-sin, zero], 0.0), pad([zero, sin], 0.0)])


def _qkv_project(x, w, split, l1, l2, tm=512):
    T, K = x.shape
    N = w.shape[1]
    tn = N // 3
    tm = _tile(math.gcd(l1, l2), tm)

    def pos_block(m):
        start = m * tm
        pos0 = jnp.where(start < split, lax.rem(start, l1), lax.rem(jnp.maximum(start - split, 0), l2))
        return pos0 // tm

    return pl.pallas_call(
        functools.partial(_qkv_kernel, n_rope_slabs=2),
        grid=(3, T // tm),
        in_specs=[pl.BlockSpec((tm, K), lambda n, m: (m, 0)),
                  pl.BlockSpec((K, tn), lambda n, m: (0, n)),
                  pl.BlockSpec((3, tm, LANES), lambda n, m: (0, pos_block(m), 0))],
        out_specs=pl.BlockSpec((tm, tn), lambda n, m: (m, n)),
        out_shape=jax.ShapeDtypeStruct((T, N), BF16),
        compiler_params=_params(("arbitrary", "arbitrary")),
        name="qkv_project",
    )(x, w, _rope_tables(max(l1, l2)))


def _pack_bf16_pairs(y):
    half = y.shape[1] // 2
    r = pltpu.bitcast(y.astype(BF16).astype(F32), jnp.uint32)
    return (r[:, :half] & jnp.uint32(0xFFFF0000)) | (r[:, half:] >> 16)


def _unpack_bf16_pairs(w):
    hi = pltpu.bitcast(w & jnp.uint32(0xFFFF0000), F32)
    lo = pltpu.bitcast(w << 16, F32)
    return jnp.concatenate([hi, lo], axis=1).astype(BF16)


def _out_ln_kernel(a_ref, w_ref, b_ref, x_ref, g_ref, beta_ref, o_ref, p_ref):
    h = jnp.dot(a_ref[...], w_ref[...], preferred_element_type=F32) + b_ref[...]
    y = _layer_norm(ALPHA * x_ref[...] + h, g_ref[...], beta_ref[...])
    o_ref[...] = y
    p_ref[...] = _pack_bf16_pairs(y)


def _out_project_ln(a, w, b, x, g, beta, tm=512):
    T, K = a.shape
    D = w.shape[1]
    tm = _tile(T, tm)
    row = lambda i: (i, 0)
    fixed = lambda i: (0, 0)
    return pl.pallas_call(
        _out_ln_kernel,
        grid=(T // tm,),
        in_specs=[pl.BlockSpec((tm, K), row), pl.BlockSpec((K, D), fixed), pl.BlockSpec((1, D), fixed),
                  pl.BlockSpec((tm, D), row), pl.BlockSpec((1, D), fixed), pl.BlockSpec((1, D), fixed)],
        out_specs=[pl.BlockSpec((tm, D), row), pl.BlockSpec((tm, D // 2), row)],
        out_shape=[jax.ShapeDtypeStruct((T, D), F32), jax.ShapeDtypeStruct((T, D // 2), jnp.uint32)],
        compiler_params=_params(("arbitrary",)),
        name="out_project_ln",
    )(a, w, b.reshape(1, D), x, g.reshape(1, D), beta.reshape(1, D))


def _filter_kernel(z_ref, w_in_ref, b_in_ref, w_mid_ref, b_mid_ref, freq_ref, w_out_ref, delta_ref,
                   k_ref, nrm_ref, *, tl):
    r = pl.program_id(1)
    z = z_ref[...]
    h = jnp.sin(freq_ref[0:1, :] * (_dot3(z, w_in_ref[...]) + b_in_ref[...]))
    for j in range(FILTER_INNER):
        h = jnp.sin(freq_ref[j + 1:j + 2, :] * (_dot3(h, w_mid_ref[j]) + b_mid_ref[j:j + 1, :]))
    out = _dot3(h, w_out_ref[...]) * jnp.exp(-z[:, 0:1] * delta_ref[...])
    row = lax.broadcasted_iota(jnp.int32, out.shape, 0) + r * tl
    out = jnp.where(row == 0, 0.0, out)
    k_ref[0] = out

    @pl.when(r == 0)
    def _():
        nrm_ref[...] = jnp.zeros_like(nrm_ref)

    nrm_ref[0] = nrm_ref[0] + jnp.sum(jnp.abs(out), axis=0, keepdims=True)


def _hyena_filters(L, D, w_in, b_in, w_mid, b_mid, freq, w_out, tl=256):
    t = jnp.linspace(0.0, 1.0, L, dtype=F32)[:, None]
    omega = (2.0 * math.pi / L) * jnp.arange(L, dtype=F32)
    bands = jnp.linspace(1e-4, FILTER_BANDS - 1, FILTER_BANDS, dtype=F32)
    phase = omega[:, None] * bands[None, :]
    z = jnp.concatenate([t, jnp.cos(phase), -jnp.sin(phase)], axis=-1)
    z_lag = jnp.concatenate([z[:1], z[1:][::-1], z], axis=0)
    deltas = jnp.abs(jnp.linspace(MIN_DECAY, MAX_DECAY, D, dtype=F32)).reshape(1, D)
    tl = _tile(L, tl)
    n_neg = L // tl
    hid = w_in.shape[1]
    fixed2 = lambda n, r: (0, 0)
    return pl.pallas_call(
        functools.partial(_filter_kernel, tl=tl),
        grid=(HYENA_ORDER, 2 * L // tl),
        in_specs=[pl.BlockSpec((tl, FILTER_EMB), lambda n, r: (r, 0)),
                  pl.BlockSpec((FILTER_EMB, hid), fixed2), pl.BlockSpec((1, hid), fixed2),
                  pl.BlockSpec((FILTER_INNER, hid, hid), lambda n, r: (0, 0, 0)),
                  pl.BlockSpec((FILTER_INNER, hid), fixed2),
                  pl.BlockSpec((FILTER_INNER + 1, hid), fixed2),
                  pl.BlockSpec((hid, D), lambda n, r: (0, 2 * n + (r < n_neg).astype(jnp.int32))),
                  pl.BlockSpec((1, D), fixed2)],
        out_specs=[pl.BlockSpec((1, tl, D), lambda n, r: (n, r, 0)),
                   pl.BlockSpec((1, SUBLANES, D), lambda n, r: (n, 0, 0))],
        out_shape=[jax.ShapeDtypeStruct((HYENA_ORDER, 2 * L, D), F32),
                   jax.ShapeDtypeStruct((HYENA_ORDER, SUBLANES, D), F32)],
        compiler_params=_params(("arbitrary", "arbitrary")),
        name="hyena_filter",
    )(z_lag, w_in, b_in.reshape(1, hid), w_mid, b_mid, freq, w_out, deltas)


def _dft_matrices(bk):
    nfp = (bk + 1 + SUBLANES - 1) // SUBLANES * SUBLANES
    f = jnp.arange(nfp, dtype=jnp.int32)[:, None]
    r = jnp.arange(bk, dtype=jnp.int32)[None, :]
    ang = (2.0 * math.pi / (2 * bk)) * ((f * r) % (2 * bk)).astype(F32)
    live = f <= bk
    cos = jnp.where(live, jnp.cos(ang), 0.0)
    sin = jnp.where(live, jnp.sin(ang), 0.0)
    fwd = jnp.concatenate([cos, -sin], axis=0)
    wgt = jnp.where((f == 0) | (f == bk), 1.0, 2.0) / (2 * bk)
    inv = jnp.concatenate([cos * wgt, -sin * wgt], axis=0).T
    return fwd.astype(BF16), inv.astype(BF16), nfp


def _spectra_kernel(k_ref, nrm_ref, f_ref, c_ref, a_prev, k_prev, *, nfp):
    s = pl.program_id(2)
    kb = k_ref[0]
    a = jnp.dot(f_ref[...], kb.astype(BF16), preferred_element_type=F32)

    @pl.when(s > 0)
    def _():
        row = lax.broadcasted_iota(jnp.int32, a.shape, 0)
        sign = jnp.where((row & 1) == 0, 1.0, -1.0)
        k0 = jnp.where(row < nfp, k_prev[0:1, :], 0.0)
        c_ref[0, 0] = (a + sign * (a_prev[...] - k0)) / nrm_ref[0, 0:1, :]

    a_prev[...] = a
    k_prev[...] = kb[0:SUBLANES, :]


def _hyena_spectra(kappa, nrm, fwd, bk, nfp, dt=256):
    order, two_l, D = kappa.shape
    n_blk = two_l // bk
    dt = _tile(D, dt)
    return pl.pallas_call(
        functools.partial(_spectra_kernel, nfp=nfp),
        grid=(order, D // dt, n_blk),
        in_specs=[pl.BlockSpec((1, bk, dt), lambda n, d, s: (n, s, d)),
                  pl.BlockSpec((1, SUBLANES, dt), lambda n, d, s: (n, 0, d)),
                  pl.BlockSpec((2 * nfp, bk), lambda n, d, s: (0, 0))],
        out_specs=pl.BlockSpec((1, 1, 2 * nfp, dt), lambda n, d, s: (n, jnp.maximum(s - 1, 0), 0, d)),
        out_shape=jax.ShapeDtypeStruct((order, n_blk - 1, 2 * nfp, D), F32),
        scratch_shapes=[pltpu.VMEM((2 * nfp, dt), F32), pltpu.VMEM((SUBLANES, dt), F32)],
        compiler_params=_params(("arbitrary", "arbitrary", "arbitrary")),
        name="hyena_spectra",
    )(kappa, nrm, fwd)


def _short_conv_block(src_ref, w_ref, b_ref, j, nb, bk):
    cur = src_ref[0, j * bk:(j + 1) * bk, :]
    row = lax.broadcasted_iota(jnp.int32, cur.shape, 0)
    if j > 0:
        before = src_ref[0, j * bk - SUBLANES:j * bk, :][SUBLANES - 1:SUBLANES, :]
    else:
        before = jnp.zeros_like(cur[0:1, :])
    if j < nb - 1:
        after = src_ref[0, (j + 1) * bk:(j + 1) * bk + SUBLANES, :][0:1, :]
    else:
        after = jnp.zeros_like(cur[0:1, :])
    up = jnp.where(row == 0, before, pltpu.roll(cur, 1, 0))
    dn = jnp.where(row == bk - 1, after, pltpu.roll(cur, bk - 1, 0))
    return b_ref[...] + up * w_ref[0:1, :] + cur * w_ref[1:2, :] + dn * w_ref[2:3, :]


def _long_conv_kernel(*refs, nb, bk, nfp, conv_input):
    if conv_input:
        (z_ref, zw_ref, zb_ref, g_ref, gw_ref, gb_ref, skip_ref, c_ref, fwd_ref, inv_ref,
         o_ref, zf_ref, y_ref) = refs
    else:
        (z_ref, g_ref, gw_ref, gb_ref, skip_ref, c_ref, fwd_ref, inv_ref, o_ref, zf_ref, y_ref) = refs

    def z_block(j):
        if conv_input:
            return _short_conv_block(z_ref, zw_ref, zb_ref, j, nb, bk)
        return z_ref[0, j * bk:(j + 1) * bk, :]

    for j in range(nb):
        zf_ref[j] = jnp.dot(fwd_ref[...], z_block(j).astype(BF16), preferred_element_type=F32)

    for i in range(nb):
        def chunk(c, carry, i=i):
            r0 = pl.multiple_of(c * SUBLANES, SUBLANES)
            re_rows, im_rows = pl.ds(r0, SUBLANES), pl.ds(nfp + r0, SUBLANES)
            acc_re = acc_im = None
            for j in range(nb):
                m = i - j + nb - 1
                c_re, c_im = c_ref[0, m, re_rows, :], c_ref[0, m, im_rows, :]
                z_re, z_im = zf_ref[j, re_rows, :], zf_ref[j, im_rows, :]
                t_re = c_re * z_re - c_im * z_im
                t_im = c_re * z_im + c_im * z_re
                acc_re = t_re if acc_re is None else acc_re + t_re
                acc_im = t_im if acc_im is None else acc_im + t_im
            y_ref[re_rows, :] = acc_re
            y_ref[im_rows, :] = acc_im
            return carry

        lax.fori_loop(0, nfp // SUBLANES, chunk, 0, unroll=max(1, 8 // nb))
        y = jnp.dot(inv_ref[...], y_ref[...].astype(BF16), preferred_element_type=F32)
        gate = _short_conv_block(g_ref, gw_ref, gb_ref, i, nb, bk)
        o_ref[0, i * bk:(i + 1) * bk, :] = (gate * (y + skip_ref[...] * z_block(i))).astype(o_ref.dtype)


def _long_conv(u3, z3, order, spectra, fwd, inv, conv_w, conv_b, skip, *, b0, nbatch, bk, nfp, dt, out_dtype):
    _, L, three_d = u3.shape
    D = three_d // 3
    nb = L // bk
    n_d = D // dt
    conv_input = z3 is None
    slab = lambda k: (lambda d, b: (b0 + b, 0, k * n_d + d))
    cols = lambda k: (lambda d, b: (0, k * n_d + d))
    seq_block = (1, L, dt)
    own = lambda d, b: (b, 0, d)
    in_specs, args = [], []
    if conv_input:
        in_specs += [pl.BlockSpec(seq_block, slab(0)), pl.BlockSpec((SHORT_CONV, dt), cols(0)),
                     pl.BlockSpec((1, dt), cols(0))]
        args += [u3, conv_w, conv_b]
    else:
        in_specs += [pl.BlockSpec(seq_block, own)]
        args += [z3]
    in_specs += [pl.BlockSpec(seq_block, slab(order + 1)), pl.BlockSpec((SHORT_CONV, dt), cols(order + 1)),
                 pl.BlockSpec((1, dt), cols(order + 1)),
                 pl.BlockSpec((1, dt), lambda d, b: (0, d)),
                 pl.BlockSpec((1, 2 * nb - 1, 2 * nfp, dt), lambda d, b: (order, 0, 0, d),
                              pipeline_mode=pl.Buffered(1)),
                 pl.BlockSpec((2 * nfp, bk), lambda d, b: (0, 0)),
                 pl.BlockSpec((bk, 2 * nfp), lambda d, b: (0, 0))]
    args += [u3, conv_w, conv_b, skip[order:order + 1], spectra, fwd, inv]
    return pl.pallas_call(
        functools.partial(_long_conv_kernel, nb=nb, bk=bk, nfp=nfp, conv_input=conv_input),
        grid=(n_d, nbatch),
        in_specs=in_specs,
        out_specs=pl.BlockSpec(seq_block, own),
        out_shape=jax.ShapeDtypeStruct((nbatch, L, D), out_dtype),
        scratch_shapes=[pltpu.VMEM((nb, 2 * nfp, dt), F32), pltpu.VMEM((2 * nfp, dt), F32)],
        compiler_params=_params(("arbitrary", "arbitrary")),
        name="hyena_long_conv",
    )(*args)


def _hyena_mixer(x, groups, p):
    T, D = x.shape
    u = _project(x, p["hy_w_in"].astype(BF16), p["hy_b_in"], F32)
    conv_w, conv_b, skip = p["hy_conv_w"], p["hy_conv_b"].reshape(1, 3 * D), p["hy_filt_skip"]
    outs = []
    for first, nbatch, L in groups:
        bk = min(L, HYENA_BLOCK)
        dt = _tile(D, 256 if L <= 2048 else 128)
        fwd, inv, nfp = _dft_matrices(bk)
        kappa, nrm = _hyena_filters(L, D, p["hy_filt_w_in"], p["hy_filt_b_in"], p["hy_filt_w_mid"],
                                    p["hy_filt_b_mid"], p["hy_filt_freq"], p["hy_filt_w_out"])
        spectra = _hyena_spectra(kappa, nrm, fwd, bk, nfp)
        kw = dict(b0=first // L, nbatch=nbatch, bk=bk, nfp=nfp, dt=dt)
        u3 = u.reshape(T // L, L, 3 * D)
        z1 = _long_conv(u3, None, 0, spectra, fwd, inv, conv_w, conv_b, skip, out_dtype=F32, **kw)
        z2 = _long_conv(u3, z1, 1, spectra, fwd, inv, conv_w, conv_b, skip, out_dtype=BF16, **kw)
        outs.append(z2.reshape(nbatch * L, D))
    return jnp.concatenate(outs, axis=0)


def _attn_kernel(q_ref, k_ref, v_ref, lam_ref, g_ref, o_ref, *, lam_init, kc):
    dh = DIFF_HEAD_DIM
    lam_dots = jnp.sum(lam_ref[0:2, :] * lam_ref[2:4, :], axis=-1, keepdims=True)
    lam = jnp.exp(lam_dots[0:1, :]) - jnp.exp(lam_dots[1:2, :]) + lam_init
    q = q_ref[...]
    n_chunks = k_ref.shape[0] // kc
    state = [None, None]
    for j in range(n_chunks):
        rows = slice(j * kc, (j + 1) * kc)
        v = v_ref[rows, :]
        for c in range(2):
            s = lax.dot_general(q[:, c * dh:(c + 1) * dh], k_ref[rows, c * dh:(c + 1) * dh],
                                (((1,), (1,)), ((), ())), preferred_element_type=F32)
            top = jnp.max(s, axis=-1, keepdims=True)
            if state[c] is None:
                e = jnp.exp2(s - top)
                state[c] = (top, jnp.sum(e, axis=-1, keepdims=True),
                            jnp.dot(e.astype(BF16), v, preferred_element_type=F32))
            else:
                m_old, l_old, acc = state[c]
                m_new = jnp.maximum(m_old, top)
                shrink = jnp.exp2(m_old - m_new)
                e = jnp.exp2(s - m_new)
                state[c] = (m_new, shrink * l_old + jnp.sum(e, axis=-1, keepdims=True),
                            shrink * acc + jnp.dot(e.astype(BF16), v, preferred_element_type=F32))
    (_, l0, o0), (_, l1, o1) = state
    o = o0 * (1.0 / l0) - o1 * (lam / l1)
    o = o * lax.rsqrt(jnp.mean(o * o, axis=-1, keepdims=True) + LN_EPS) * g_ref[...]
    o_ref[...] = (o * (1.0 - lam_init)).astype(o_ref.dtype)


def _diff_attention(qkv, lam_vecs, subln_g, layer_idx, *, first, nbatch, L, tq, kc):
    T, three_d = qkv.shape
    D = three_d // 3
    hw = 2 * DIFF_HEAD_DIM
    H = D // hw
    tq, kc = _tile(L, tq), _tile(L, kc)
    nq = L // tq
    q0, b0 = first // tq, first // L
    lam_init = 0.8 - 0.6 * math.exp(-0.3 * layer_idx)
    return pl.pallas_call(
        functools.partial(_attn_kernel, lam_init=lam_init, kc=kc),
        grid=(nbatch, H, nq),
        in_specs=[pl.BlockSpec((tq, hw), lambda b, h, i: (q0 + b * nq + i, h)),
                  pl.BlockSpec((L, hw), lambda b, h, i: (b0 + b, H + h)),
                  pl.BlockSpec((L, hw), lambda b, h, i: (b0 + b, 2 * H + h)),
                  pl.BlockSpec((4, DIFF_HEAD_DIM), lambda b, h, i: (0, 0)),
                  pl.BlockSpec((1, hw), lambda b, h, i: (0, 0))],
        out_specs=pl.BlockSpec((tq, hw), lambda b, h, i: (b * nq + i, h)),
        out_shape=jax.ShapeDtypeStruct((nbatch * L, D), BF16),
        compiler_params=_params(("arbitrary", "arbitrary", "arbitrary")),
        name="diff_attention",
    )(qkv, qkv, qkv, lam_vecs, subln_g.reshape(1, hw))


def _attention_mixer(x, groups, p, layer_idx):
    split = groups[1][0] if len(groups) > 1 else x.shape[0]
    l1, l2 = groups[0][2], groups[-1][2]
    qkv = _qkv_project(x, p["da_w_qkv"].astype(BF16), split, l1, l2)
    lam_vecs = jnp.stack([p["da_lambda_q1"], p["da_lambda_q2"], p["da_lambda_k1"], p["da_lambda_k2"]])
    outs = [_diff_attention(qkv, lam_vecs, p["da_subln_g"], layer_idx, first=first, nbatch=nbatch, L=L,
                            tq=ATTN_Q_TILE * (1 if L > 2048 else 2), kc=ATTN_KEY_CHUNK // (1 if L > 2048 else 2))
            for first, nbatch, L in groups]
    return jnp.concatenate(outs, axis=0)


def _router_kernel(x_ref, w_ref, b_ref, idx_ref, gate_ref, rank_ref, cnt_ref, base_ref, *, n_experts):
    i = pl.program_id(0)

    @pl.when(i == 0)
    def _():
        base_ref[...] = jnp.zeros_like(base_ref)

    logits = _dot3(x_ref[...], w_ref[...]) + b_ref[...]
    tm = logits.shape[0]
    lane = lax.broadcasted_iota(jnp.int32, (tm, LANES), 1)
    neg_inf = jnp.float32(-jnp.inf)
    lane_f = lane.astype(F32)
    live = jnp.where(lane < n_experts, logits, neg_inf)
    vals, picks, hots = [], [], []
    for _ in range(TOP_K):
        best = jnp.max(live, axis=-1, keepdims=True)
        pick = jnp.min(jnp.where(live == best, lane_f, float(LANES)), axis=-1, keepdims=True)
        hot = lane_f == pick
        live = jnp.where(hot, neg_inf, live)
        vals.append(best)
        picks.append(pick.astype(jnp.int32))
        hots.append(hot)
    exps = [jnp.exp(v - vals[0]) for v in vals]
    denom = exps[0] + exps[1] + exps[2] + exps[3]
    chosen = jnp.zeros((tm, LANES), F32)
    for hot in hots:
        chosen = chosen + hot.astype(F32)
    r_i = lax.broadcasted_iota(jnp.int32, (tm, tm), 0)
    c_i = lax.broadcasted_iota(jnp.int32, (tm, tm), 1)
    earlier = jnp.where(r_i > c_i, 1.0, 0.0).astype(BF16)
    before = base_ref[0:1, :] + jnp.dot(earlier, chosen.astype(BF16), preferred_element_type=F32)
    idx_out = jnp.zeros((tm, LANES), jnp.int32)
    rank_out = jnp.zeros((tm, LANES), jnp.int32)
    gate_out = jnp.zeros((tm, LANES), F32)
    for k in range(TOP_K):
        rank_k = jnp.sum(jnp.where(hots[k], before, 0.0), axis=-1, keepdims=True).astype(jnp.int32)
        idx_out = jnp.where(lane == k, picks[k], idx_out)
        rank_out = jnp.where(lane == k, rank_k, rank_out)
        gate_out = jnp.where(lane == k, exps[k] / denom, gate_out)
    idx_ref[...] = idx_out
    rank_ref[...] = rank_out
    gate_ref[...] = gate_out
    base_ref[...] = base_ref[...] + jnp.sum(chosen, axis=0, keepdims=True)
    cnt_ref[...] = base_ref[...]


def _route(x, router_w, router_b, tm=512):
    T, D = x.shape
    E = router_w.shape[1]
    tm = _tile(T, tm)
    w = jnp.pad(router_w, ((0, 0), (0, LANES - E)))
    b = jnp.pad(router_b, (0, LANES - E)).reshape(1, LANES)
    row = lambda i: (i, 0)
    fixed = lambda i: (0, 0)
    wide = pl.BlockSpec((tm, LANES), row)
    return pl.pallas_call(
        functools.partial(_router_kernel, n_experts=E),
        grid=(T // tm,),
        in_specs=[pl.BlockSpec((tm, D), row), pl.BlockSpec((D, LANES), fixed), pl.BlockSpec((1, LANES), fixed)],
        out_specs=[wide, wide, wide, pl.BlockSpec((SUBLANES, LANES), fixed)],
        out_shape=[jax.ShapeDtypeStruct((T, LANES), jnp.int32), jax.ShapeDtypeStruct((T, LANES), F32),
                   jax.ShapeDtypeStruct((T, LANES), jnp.int32), jax.ShapeDtypeStruct((SUBLANES, LANES), F32)],
        scratch_shapes=[pltpu.VMEM((SUBLANES, LANES), F32)],
        compiler_params=_params(("arbitrary",)),
        name="moe_route",
    )(x, w, b)


def _row_copy(src, src_row, dst, dst_row, sem):
    return pltpu.make_async_copy(src.at[pl.ds(src_row, 1)], dst.at[pl.ds(dst_row, 1)], sem)


def _dispatch_kernel(pend_ref, padded_ref, dest_ref, x_ref, xs_out, zeros, sem, zero_sem, *, tm, rb, n_experts):
    @pl.when(pl.program_id(0) == 0)
    def _():
        zeros[...] = jnp.zeros_like(zeros)

        def last_tile(e):
            first_row = pl.multiple_of(pend_ref[e] - rb, rb)
            return pltpu.make_async_copy(zeros, xs_out.at[pl.ds(first_row, rb)], zero_sem)

        for e in range(n_experts):
            @pl.when(padded_ref[e] > 0)
            def _(e=e):
                last_tile(e).start()

        for e in range(n_experts):
            @pl.when(padded_ref[e] > 0)
            def _(e=e):
                last_tile(e).wait()

    def issue(t, carry):
        for k in range(TOP_K):
            _row_copy(x_ref, t, xs_out, dest_ref[t * TOP_K + k], sem).start(priority=k % 2)
        return carry

    def drain(t, carry):
        for k in range(TOP_K):
            _row_copy(x_ref, 0, xs_out, 0, sem).wait()
        return carry

    lax.fori_loop(0, tm, issue, 0)
    lax.fori_loop(0, tm, drain, 0)


def _dispatch(x_packed, dest_flat, padded_end, padded, n_rows, rb, tm=256):
    T, half = x_packed.shape
    tm = _tile(T, tm)
    return pl.pallas_call(
        functools.partial(_dispatch_kernel, tm=tm, rb=rb, n_experts=padded.shape[0]),
        grid_spec=pltpu.PrefetchScalarGridSpec(
            num_scalar_prefetch=2,
            grid=(T // tm,),
            in_specs=[pl.BlockSpec((tm * TOP_K,), lambda i, pe, pd: (i,), memory_space=pltpu.SMEM),
                      pl.BlockSpec((tm, half), lambda i, pe, pd: (i, 0))],
            out_specs=pl.BlockSpec(memory_space=pl.ANY),
            scratch_shapes=[pltpu.VMEM((rb, half), jnp.uint32), pltpu.SemaphoreType.DMA(()),
                            pltpu.SemaphoreType.DMA(())]),
        out_shape=jax.ShapeDtypeStruct((n_rows, half), jnp.uint32),
        compiler_params=_params(("arbitrary",)),
        name="moe_dispatch",
    )(padded_end, padded, dest_flat, x_packed)


def _expert_kernel(tile_expert_ref, n_used_ref, x_ref, wg_ref, bg_ref, wu_ref, bu_ref, wd_ref, bd_ref, o_ref):
    del tile_expert_ref

    @pl.when(pl.program_id(0) < n_used_ref[0])
    def _():
        x = _unpack_bf16_pairs(x_ref[...])
        gate = jnp.dot(x, wg_ref[0], preferred_element_type=F32) + bg_ref[0]
        up = jnp.dot(x, wu_ref[0], preferred_element_type=F32) + bu_ref[0]
        gate = jnp.minimum(gate, SWIGLU_LIMIT)
        up = jnp.clip(up, -SWIGLU_LIMIT, SWIGLU_LIMIT)
        h = gate * (1.0 / (1.0 + jnp.exp(-SWIGLU_ALPHA * gate))) * (up + 1.0)
        o_ref[...] = jnp.dot(h.astype(BF16), wd_ref[0], preferred_element_type=F32) + bd_ref[0]


def _experts(xs, tile_expert, n_used, wg, bg, wu, bu, wd, bd, rb):
    P, half = xs.shape
    E, D, F = wg.shape
    n_tiles = P // rb
    of_expert = lambda i, te, nu: (te[i], 0, 0)
    resident = dict(pipeline_mode=pl.Buffered(1))
    return pl.pallas_call(
        _expert_kernel,
        grid_spec=pltpu.PrefetchScalarGridSpec(
            num_scalar_prefetch=2,
            grid=(n_tiles,),
            in_specs=[pl.BlockSpec((rb, half), lambda i, te, nu: (jnp.minimum(i, nu[0] - 1), 0)),
                      pl.BlockSpec((1, D, F), of_expert, **resident), pl.BlockSpec((1, 1, F), of_expert),
                      pl.BlockSpec((1, D, F), of_expert, **resident), pl.BlockSpec((1, 1, F), of_expert),
                      pl.BlockSpec((1, F, D), of_expert, **resident), pl.BlockSpec((1, 1, D), of_expert)],
            out_specs=pl.BlockSpec((rb, D), lambda i, te, nu: (i, 0))),
        out_shape=jax.ShapeDtypeStruct((P, D), F32),
        compiler_params=_params(("arbitrary",)),
        name="moe_experts",
    )(tile_expert, n_used, xs, wg, bg, wu, bu, wd, bd)


def _combine_kernel(dest_ref, dest_next_ref, gate_ref, x_ref, g_ref, beta_ref, ys_hbm, *rest,
                    tc, n_steps, n_first):
    *o_refs, rows, sem = rest
    i = pl.program_id(0)
    slot = lax.rem(i, 2)

    def gather(d_ref, s):
        def issue(t, carry):
            for k in range(TOP_K):
                _row_copy(ys_hbm, d_ref[t * TOP_K + k], rows.at[s, k], t, sem.at[s]).start(priority=k % 2)
            return carry
        lax.fori_loop(0, tc, issue, 0)

    @pl.when(i == 0)
    def _():
        gather(dest_ref, 0)

    @pl.when(i + 1 < n_steps)
    def _():
        gather(dest_next_ref, 1 - slot)

    def drain(t, carry):
        for k in range(TOP_K):
            _row_copy(ys_hbm, 0, rows.at[slot, k], 0, sem.at[slot]).wait()
        return carry

    lax.fori_loop(0, tc, drain, 0)
    gates = gate_ref[...]
    mix = gates[:, 0:1] * rows[slot, 0]
    for k in range(1, TOP_K):
        mix = mix + gates[:, k:k + 1] * rows[slot, k]
    y = _layer_norm(ALPHA * x_ref[...] + mix, g_ref[...], beta_ref[...])
    if n_first is None:
        o_refs[0][...] = y
    else:
        @pl.when(i < n_first)
        def _():
            o_refs[0][...] = y

        @pl.when(i >= n_first)
        def _():
            o_refs[1][...] = y


def _combine_ln(ys, dest_flat, gates, x, g, beta, split=None, tc=128):
    T, D = x.shape
    tc = _tile(T if split is None else math.gcd(split, T - split), tc)
    n_steps = T // tc
    row = lambda i: (i, 0)
    fixed = lambda i: (0, 0)
    tile = (tc, D)
    if split is None:
        n_first = None
        out_specs = pl.BlockSpec(tile, row)
        out_shape = jax.ShapeDtypeStruct((T, D), F32)
    else:
        n_first = split // tc
        out_specs = [pl.BlockSpec(tile, lambda i: (jnp.minimum(i, n_first - 1), 0)),
                     pl.BlockSpec(tile, lambda i: (jnp.maximum(i - n_first, 0), 0))]
        out_shape = [jax.ShapeDtypeStruct((split, D), F32), jax.ShapeDtypeStruct((T - split, D), F32)]
    return pl.pallas_call(
        functools.partial(_combine_kernel, tc=tc, n_steps=n_steps, n_first=n_first),
        grid=(n_steps,),
        in_specs=[pl.BlockSpec((tc * TOP_K,), lambda i: (i,), memory_space=pltpu.SMEM),
                  pl.BlockSpec((tc * TOP_K,), lambda i: (jnp.minimum(i + 1, n_steps - 1),),
                               memory_space=pltpu.SMEM),
                  pl.BlockSpec((tc, LANES), row), pl.BlockSpec(tile, row),
                  pl.BlockSpec((1, D), fixed), pl.BlockSpec((1, D), fixed),
                  pl.BlockSpec(memory_space=pl.ANY)],
        out_specs=out_specs,
        out_shape=out_shape,
        scratch_shapes=[pltpu.VMEM((2, TOP_K, tc, D), F32), pltpu.SemaphoreType.DMA((2,))],
        compiler_params=_params(("arbitrary",)),
        name="moe_combine_ln",
    )(dest_flat, dest_flat, gates, x, g.reshape(1, D), beta.reshape(1, D), ys)


def _expert_stacks(p):
    n_layers, E, D, F = p["moe_w_gate"].shape
    stack = lambda w: w.astype(BF16).reshape((n_layers * E,) + w.shape[2:])
    bias = lambda b: b.reshape(n_layers * E, 1, b.shape[-1])
    return (stack(p["moe_w_gate"]), bias(p["moe_b_gate"]), stack(p["moe_w_up"]), bias(p["moe_b_up"]),
            stack(p["moe_w_down"]), bias(p["moe_b_down"]))


def _moe_ln(x, x_packed, p, stacks, i, rb, split=None):
    T, D = x.shape
    E = p["moe_router_w"].shape[-1]
    idx, gates, rank, counts = _route(x, p["moe_router_w"][i], p["moe_router_b"][i])
    counts = counts[0, :E].astype(jnp.int32)
    padded = (counts + rb - 1) // rb * rb
    padded_end = jnp.cumsum(padded)
    padded_start = padded_end - padded
    n_tiles = (T * TOP_K) // rb + E
    tile_expert = jnp.minimum(jnp.searchsorted(padded_end, jnp.arange(n_tiles, dtype=jnp.int32) * rb, side="right"),
                              E - 1).astype(jnp.int32)
    n_used = (padded_end[-1:] // rb).astype(jnp.int32)
    dest = (padded_start[idx[:, :TOP_K]] + rank[:, :TOP_K]).reshape(T * TOP_K)
    xs = _dispatch(x_packed, dest, padded_end.astype(jnp.int32), padded, n_tiles * rb, rb)
    ys = _experts(xs, tile_expert + i * E, n_used, *stacks, rb)
    return _combine_ln(ys, dest, gates, x, p["ln2_g"][i], p["ln2_b"][i], split)


def _encoder(x, groups, p, rb):
    stacks = _expert_stacks(p)
    for i in range(DEPTH):
        j = i // 2
        if i % 2 == 0:
            pj = {k: v[j] for k, v in p.items() if k.startswith("hy_")}
            a = _hyena_mixer(x, groups, pj)
            w_o, b_o = pj["hy_w_out"], pj["hy_b_out"]
        else:
            pj = {k: v[j] for k, v in p.items() if k.startswith("da_")}
            a = _attention_mixer(x, groups, pj, i)
            w_o, b_o = pj["da_w_o"], jnp.zeros((x.shape[1],), F32)
        x, x_packed = _out_project_ln(a, w_o.astype(BF16), b_o, x, p["ln1_g"][i], p["ln1_b"][i])
        last = i == DEPTH - 1
        x = _moe_ln(x, x_packed, p, stacks, i, rb, groups[1][0] if last and len(groups) > 1 else None)
    return x


def _forward(x_prompt, x_sample, p, rb=512):
    B1, L1, D = x_prompt.shape
    B2, L2, _ = x_sample.shape
    T1, T2 = B1 * L1, B2 * L2
    assert T1 % L2 == 0, "sample sequences must start on a multiple of their length"
    x = jnp.concatenate([x_prompt.reshape(T1, D), x_sample.reshape(T2, D)], axis=0)
    y1, y2 = _encoder(x, [(0, B1, L1), (T1, B2, L2)], p, rb)
    return y1.reshape(B1, L1, D), y2.reshape(B2, L2, D)


def kernel(x_prompt, x_sample, hy_w_in, hy_b_in, hy_conv_w, hy_conv_b, hy_filt_w_in, hy_filt_b_in, hy_filt_w_mid, hy_filt_b_mid, hy_filt_freq, hy_filt_w_out, hy_filt_skip, hy_w_out, hy_b_out, da_w_qkv, da_lambda_q1, da_lambda_k1, da_lambda_q2, da_lambda_k2, da_subln_g, da_w_o, ln1_g, ln1_b, moe_router_w, moe_router_b, moe_w_gate, moe_b_gate, moe_w_up, moe_b_up, moe_w_down, moe_b_down, ln2_g, ln2_b):
    p = dict(hy_w_in=hy_w_in, hy_b_in=hy_b_in, hy_conv_w=hy_conv_w, hy_conv_b=hy_conv_b,
             hy_filt_w_in=hy_filt_w_in, hy_filt_b_in=hy_filt_b_in, hy_filt_w_mid=hy_filt_w_mid,
             hy_filt_b_mid=hy_filt_b_mid, hy_filt_freq=hy_filt_freq, hy_filt_w_out=hy_filt_w_out,
             hy_filt_skip=hy_filt_skip, hy_w_out=hy_w_out, hy_b_out=hy_b_out,
             da_w_qkv=da_w_qkv, da_lambda_q1=da_lambda_q1, da_lambda_k1=da_lambda_k1,
             da_lambda_q2=da_lambda_q2, da_lambda_k2=da_lambda_k2, da_subln_g=da_subln_g, da_w_o=da_w_o,
             ln1_g=ln1_g, ln1_b=ln1_b, moe_router_w=moe_router_w, moe_router_b=moe_router_b,
             moe_w_gate=moe_w_gate, moe_b_gate=moe_b_gate, moe_w_up=moe_w_up, moe_b_up=moe_b_up,
             moe_w_down=moe_w_down, moe_b_down=moe_b_down, ln2_g=ln2_g, ln2_b=ln2_b)
    return _forward(x_prompt, x_sample, p)
```

```python
import functools
import math

import jax
import jax.numpy as jnp
from jax import lax
from jax.experimental import pallas as pl
from jax.experimental.pallas import tpu as pltpu

DEPTH = 2
SHORT_CONV = 3
HYENA_ORDER = 2
FILTER_EMB = 33
FILTER_BANDS = (FILTER_EMB - 1) // 2
FILTER_INNER = 2
FAST_DECAY_PCT = 0.3
SLOW_DECAY_PCT = 1.5
DECAY_TARGET = 1e-2
MAX_DECAY = math.log(DECAY_TARGET) / FAST_DECAY_PCT
MIN_DECAY = math.log(DECAY_TARGET) / SLOW_DECAY_PCT
DIFF_HEAD_DIM = 128
ROPE_DIM = DIFF_HEAD_DIM // 4
ROPE_THETA = 500000.0
TOP_K = 4
SWIGLU_ALPHA = 1.702
SWIGLU_LIMIT = 7.0
LN_EPS = 1e-5
ALPHA = (2 * DEPTH) ** 0.25

LANES = 128
SUBLANES = 8
VMEM_LIMIT_BYTES = 60 * 1024 * 1024

HYENA_BLOCK = 512

ATTN_Q_TILE = 512
ATTN_KEY_CHUNK = 2048

F32 = jnp.float32
BF16 = jnp.bfloat16


def _params(semantics, vmem=VMEM_LIMIT_BYTES):
    return pltpu.CompilerParams(dimension_semantics=semantics, vmem_limit_bytes=vmem)


def _tile(n, pref):
    t = min(n, pref)
    assert n % t == 0, (n, t)
    return t


def _split_bf16(a):
    hi = a.astype(BF16)
    lo = (a - hi.astype(F32)).astype(BF16)
    return hi, lo


def _dot3(a, b):
    ah, al = _split_bf16(a)
    bh, bl = _split_bf16(b)
    d = functools.partial(jnp.dot, preferred_element_type=F32)
    return d(ah, bh) + d(al, bh) + d(ah, bl)


def _layer_norm(y, g, b):
    mu = jnp.mean(y, axis=-1, keepdims=True)
    yc = y - mu
    var = jnp.mean(yc * yc, axis=-1, keepdims=True)
    return yc * lax.rsqrt(var + LN_EPS) * g + b


def _proj_kernel(x_ref, w_ref, b_ref, o_ref):
    acc = jnp.dot(x_ref[...].astype(BF16), w_ref[...], preferred_element_type=F32)
    o_ref[...] = (acc + b_ref[...]).astype(o_ref.dtype)


def _project(x, w, b, out_dtype, tm=512, tn=2048):
    T, K = x.shape
    N = w.shape[1]
    tm, tn = _tile(T, tm), _tile(N, tn)
    return pl.pallas_call(
        _proj_kernel,
        grid=(N // tn, T // tm),
        in_specs=[pl.BlockSpec((tm, K), lambda n, m: (m, 0)),
                  pl.BlockSpec((K, tn), lambda n, m: (0, n)),
                  pl.BlockSpec((1, tn), lambda n, m: (0, n))],
        out_specs=pl.BlockSpec((tm, tn), lambda n, m: (m, n)),
        out_shape=jax.ShapeDtypeStruct((T, N), out_dtype),
        compiler_params=_params(("arbitrary", "arbitrary")),
        name="project",
    )(x, w, b.reshape(1, N))


def _qkv_kernel(x_ref, w_ref, rope_ref, o_ref, *, n_rope_slabs):
    n = pl.program_id(0)
    acc = jnp.dot(x_ref[...].astype(BF16), w_ref[...], preferred_element_type=F32)
    tn = acc.shape[1]

    @pl.when(n >= n_rope_slabs)
    def _():
        o_ref[...] = acc.astype(o_ref.dtype)

    @pl.when(n < n_rope_slabs)
    def _():
        scale = jnp.where(n == 0, DIFF_HEAD_DIM ** -0.5 * math.log2(math.e), 1.0)
        c_tab, s_lo, s_hi = rope_ref[0] * scale, rope_ref[1] * scale, rope_ref[2] * scale
        half = ROPE_DIM // 2
        for h in range(tn // LANES):
            xh = acc[:, h * LANES:(h + 1) * LANES]
            rot = (xh * c_tab + pltpu.roll(xh, LANES - half, 1) * s_lo
                   + pltpu.roll(xh, half, 1) * s_hi)
            o_ref[:, h * LANES:(h + 1) * LANES] = rot.astype(o_ref.dtype)


def _rope_tables(n_pos):
    half = ROPE_DIM // 2
    inv_freq = ROPE_THETA ** (-jnp.arange(0, ROPE_DIM, 2, dtype=F32) / ROPE_DIM)
    ang = jnp.arange(n_pos, dtype=F32)[:, None] * inv_freq[None, :]
    cos, sin = jnp.cos(ang), jnp.sin(ang)
    rest = LANES - ROPE_DIM
    pad = lambda parts, fill: jnp.concatenate(parts + [jnp.full((n_pos, rest), fill, F32)], axis=1)
    zero = jnp.zeros_like(sin)
    return jnp.stack([pad([cos, cos], 1.0), pad([-sin, zero], 0.0), pad([zero, sin], 0.0)])


def _qkv_project(x, w, split, l1, l2, tm=512):
    T, K = x.shape
    N = w.shape[1]
    tn = N // 3
    tm = _tile(math.gcd(l1, l2), tm)

    def pos_block(m):
        start = m * tm
        pos0 = jnp.where(start < split, lax.rem(start, l1), lax.rem(jnp.maximum(start - split, 0), l2))
        return pos0 // tm

    return pl.pallas_call(
        functools.partial(_qkv_kernel, n_rope_slabs=2),
        grid=(3, T // tm),
        in_specs=[pl.BlockSpec((tm, K), lambda n, m: (m, 0)),
                  pl.BlockSpec((K, tn), lambda n, m: (0, n)),
                  pl.BlockSpec((3, tm, LANES), lambda n, m: (0, pos_block(m), 0))],
        out_specs=pl.BlockSpec((tm, tn), lambda n, m: (m, n)),
        out_shape=jax.ShapeDtypeStruct((T, N), BF16),
        compiler_params=_params(("arbitrary", "arbitrary")),
        name="qkv_project",
    )(x, w, _rope_tables(max(l1, l2)))


def _pack_bf16_pairs(y):
    half = y.shape[1] // 2
    r = pltpu.bitcast(y.astype(BF16).astype(F32), jnp.uint32)
    return (r[:, :half] & jnp.uint32(0xFFFF0000)) | (r[:, half:] >> 16)


def _unpack_bf16_pairs(w):
    hi = pltpu.bitcast(w & jnp.uint32(0xFFFF0000), F32)
    lo = pltpu.bitcast(w << 16, F32)
    return jnp.concatenate([hi, lo], axis=1).astype(BF16)


def _out_ln_kernel(a_ref, w_ref, b_ref, x_ref, g_ref, beta_ref, o_ref, p_ref):
    h = jnp.dot(a_ref[...], w_ref[...], preferred_element_type=F32) + b_ref[...]
    y = _layer_norm(ALPHA * x_ref[...] + h, g_ref[...], beta_ref[...])
    o_ref[...] = y
    p_ref[...] = _pack_bf16_pairs(y)


def _out_project_ln(a, w, b, x, g, beta, tm=512):
    T, K = a.shape
    D = w.shape[1]
    tm = _tile(T, tm)
    row = lambda i: (i, 0)
    fixed = lambda i: (0, 0)
    return pl.pallas_call(
        _out_ln_kernel,
        grid=(T // tm,),
        in_specs=[pl.BlockSpec((tm, K), row), pl.BlockSpec((K, D), fixed), pl.BlockSpec((1, D), fixed),
                  pl.BlockSpec((tm, D), row), pl.BlockSpec((1, D), fixed), pl.BlockSpec((1, D), fixed)],
        out_specs=[pl.BlockSpec((tm, D), row), pl.BlockSpec((tm, D // 2), row)],
        out_shape=[jax.ShapeDtypeStruct((T, D), F32), jax.ShapeDtypeStruct((T, D // 2), jnp.uint32)],
        compiler_params=_params(("arbitrary",)),
        name="out_project_ln",
    )(a, w, b.reshape(1, D), x, g.reshape(1, D), beta.reshape(1, D))


def _filter_kernel(z_ref, w_in_ref, b_in_ref, w_mid_ref, b_mid_ref, freq_ref, w_out_ref, delta_ref,
                   k_ref, nrm_ref, *, tl):
    r = pl.program_id(1)
    z = z_ref[...]
    h = jnp.sin(freq_ref[0:1, :] * (_dot3(z, w_in_ref[...]) + b_in_ref[...]))
    for j in range(FILTER_INNER):
        h = jnp.sin(freq_ref[j + 1:j + 2, :] * (_dot3(h, w_mid_ref[j]) + b_mid_ref[j:j + 1, :]))
    out = _dot3(h, w_out_ref[...]) * jnp.exp(-z[:, 0:1] * delta_ref[...])
    row = lax.broadcasted_iota(jnp.int32, out.shape, 0) + r * tl
    out = jnp.where(row == 0, 0.0, out)
    k_ref[0] = out

    @pl.when(r == 0)
    def _():
        nrm_ref[...] = jnp.zeros_like(nrm_ref)

    nrm_ref[0] = nrm_ref[0] + jnp.sum(jnp.abs(out), axis=0, keepdims=True)


def _hyena_filters(L, D, w_in, b_in, w_mid, b_mid, freq, w_out, tl=256):
    t = jnp.linspace(0.0, 1.0, L, dtype=F32)[:, None]
    omega = (2.0 * math.pi / L) * jnp.arange(L, dtype=F32)
    bands = jnp.linspace(1e-4, FILTER_BANDS - 1, FILTER_BANDS, dtype=F32)
    phase = omega[:, None] * bands[None, :]
    z = jnp.concatenate([t, jnp.cos(phase), -jnp.sin(phase)], axis=-1)
    z_lag = jnp.concatenate([z[:1], z[1:][::-1], z], axis=0)
    deltas = jnp.abs(jnp.linspace(MIN_DECAY, MAX_DECAY, D, dtype=F32)).reshape(1, D)
    tl = _tile(L, tl)
    n_neg = L // tl
    hid = w_in.shape[1]
    fixed2 = lambda n, r: (0, 0)
    return pl.pallas_call(
        functools.partial(_filter_kernel, tl=tl),
        grid=(HYENA_ORDER, 2 * L // tl),
        in_specs=[pl.BlockSpec((tl, FILTER_EMB), lambda n, r: (r, 0)),
                  pl.BlockSpec((FILTER_EMB, hid), fixed2), pl.BlockSpec((1, hid), fixed2),
                  pl.BlockSpec((FILTER_INNER, hid, hid), lambda n, r: (0, 0, 0)),
                  pl.BlockSpec((FILTER_INNER, hid), fixed2),
                  pl.BlockSpec((FILTER_INNER + 1, hid), fixed2),
                  pl.BlockSpec((hid, D), lambda n, r: (0, 2 * n + (r < n_neg).astype(jnp.int32))),
                  pl.BlockSpec((1, D), fixed2)],
        out_specs=[pl.BlockSpec((1, tl, D), lambda n, r: (n, r, 0)),
                   pl.BlockSpec((1, SUBLANES, D), lambda n, r: (n, 0, 0))],
        out_shape=[jax.ShapeDtypeStruct((HYENA_ORDER, 2 * L, D), F32),
                   jax.ShapeDtypeStruct((HYENA_ORDER, SUBLANES, D), F32)],
        compiler_params=_params(("arbitrary", "arbitrary")),
        name="hyena_filter",
    )(z_lag, w_in, b_in.reshape(1, hid), w_mid, b_mid, freq, w_out, deltas)


def _dft_matrices(bk):
    nfp = (bk + 1 + SUBLANES - 1) // SUBLANES * SUBLANES
    f = jnp.arange(nfp, dtype=jnp.int32)[:, None]
    r = jnp.arange(bk, dtype=jnp.int32)[None, :]
    ang = (2.0 * math.pi / (2 * bk)) * ((f * r) % (2 * bk)).astype(F32)
    live = f <= bk
    cos = jnp.where(live, jnp.cos(ang), 0.0)
    sin = jnp.where(live, jnp.sin(ang), 0.0)
    fwd = jnp.concatenate([cos, -sin], axis=0)
    wgt = jnp.where((f == 0) | (f == bk), 1.0, 2.0) / (2 * bk)
    inv = jnp.concatenate([cos * wgt, -sin * wgt], axis=0).T
    return fwd.astype(BF16), inv.astype(BF16), nfp


def _spectra_kernel(k_ref, nrm_ref, f_ref, c_ref, a_prev, k_prev, *, nfp):
    s = pl.program_id(2)
    kb = k_ref[0]
    a = jnp.dot(f_ref[...], kb.astype(BF16), preferred_element_type=F32)

    @pl.when(s > 0)
    def _():
        row = lax.broadcasted_iota(jnp.int32, a.shape, 0)
        sign = jnp.where((row & 1) == 0, 1.0, -1.0)
        k0 = jnp.where(row < nfp, k_prev[0:1, :], 0.0)
        c_ref[0, 0] = (a + sign * (a_prev[...] - k0)) / nrm_ref[0, 0:1, :]

    a_prev[...] = a
    k_prev[...] = kb[0:SUBLANES, :]


def _hyena_spectra(kappa, nrm, fwd, bk, nfp, dt=256):
    order, two_l, D = kappa.shape
    n_blk = two_l // bk
    dt = _tile(D, dt)
    return pl.pallas_call(
        functools.partial(_spectra_kernel, nfp=nfp),
        grid=(order, D // dt, n_blk),
        in_specs=[pl.BlockSpec((1, bk, dt), lambda n, d, s: (n, s, d)),
                  pl.BlockSpec((1, SUBLANES, dt), lambda n, d, s: (n, 0, d)),
                  pl.BlockSpec((2 * nfp, bk), lambda n, d, s: (0, 0))],
        out_specs=pl.BlockSpec((1, 1, 2 * nfp, dt), lambda n, d, s: (n, jnp.maximum(s - 1, 0), 0, d)),
        out_shape=jax.ShapeDtypeStruct((order, n_blk - 1, 2 * nfp, D), F32),
        scratch_shapes=[pltpu.VMEM((2 * nfp, dt), F32), pltpu.VMEM((SUBLANES, dt), F32)],
        compiler_params=_params(("arbitrary", "arbitrary", "arbitrary")),
        name="hyena_spectra",
    )(kappa, nrm, fwd)


def _short_conv_block(src_ref, w_ref, b_ref, j, nb, bk):
    cur = src_ref[0, j * bk:(j + 1) * bk, :]
    row = lax.broadcasted_iota(jnp.int32, cur.shape, 0)
    if j > 0:
        before = src_ref[0, j * bk - SUBLANES:j * bk, :][SUBLANES - 1:SUBLANES, :]
    else:
        before = jnp.zeros_like(cur[0:1, :])
    if j < nb - 1:
        after = src_ref[0, (j + 1) * bk:(j + 1) * bk + SUBLANES, :][0:1, :]
    else:
        after = jnp.zeros_like(cur[0:1, :])
    up = jnp.where(row == 0, before, pltpu.roll(cur, 1, 0))
    dn = jnp.where(row == bk - 1, after, pltpu.roll(cur, bk - 1, 0))
    return b_ref[...] + up * w_ref[0:1, :] + cur * w_ref[1:2, :] + dn * w_ref[2:3, :]


def _long_conv_kernel(*refs, nb, bk, nfp, conv_input):
    if conv_input:
        (z_ref, zw_ref, zb_ref, g_ref, gw_ref, gb_ref, skip_ref, c_ref, fwd_ref, inv_ref,
         o_ref, zf_ref, y_ref) = refs
    else:
        (z_ref, g_ref, gw_ref, gb_ref, skip_ref, c_ref, fwd_ref, inv_ref, o_ref, zf_ref, y_ref) = refs

    def z_block(j):
        if conv_input:
            return _short_conv_block(z_ref, zw_ref, zb_ref, j, nb, bk)
        return z_ref[0, j * bk:(j + 1) * bk, :]

    for j in range(nb):
        zf_ref[j] = jnp.dot(fwd_ref[...], z_block(j).astype(BF16), preferred_element_type=F32)

    for i in range(nb):
        def chunk(c, carry, i=i):
            r0 = pl.multiple_of(c * SUBLANES, SUBLANES)
            re_rows, im_rows = pl.ds(r0, SUBLANES), pl.ds(nfp + r0, SUBLANES)
            acc_re = acc_im = None
            for j in range(nb):
                m = i - j + nb - 1
                c_re, c_im = c_ref[0, m, re_rows, :], c_ref[0, m, im_rows, :]
                z_re, z_im = zf_ref[j, re_rows, :], zf_ref[j, im_rows, :]
                t_re = c_re * z_re - c_im * z_im
                t_im = c_re * z_im + c_im * z_re
                acc_re = t_re if acc_re is None else acc_re + t_re
                acc_im = t_im if acc_im is None else acc_im + t_im
            y_ref[re_rows, :] = acc_re
            y_ref[im_rows, :] = acc_im
            return carry

        lax.fori_loop(0, nfp // SUBLANES, chunk, 0, unroll=max(1, 8 // nb))
        y = jnp.dot(inv_ref[...], y_ref[...].astype(BF16), preferred_element_type=F32)
        gate = _short_conv_block(g_ref, gw_ref, gb_ref, i, nb, bk)
        o_ref[0, i * bk:(i + 1) * bk, :] = (gate * (y + skip_ref[...] * z_block(i))).astype(o_ref.dtype)


def _long_conv(u3, z3, order, spectra, fwd, inv, conv_w, conv_b, skip, *, b0, nbatch, bk, nfp, dt, out_dtype):
    _, L, three_d = u3.shape
    D = three_d // 3
    nb = L // bk
    n_d = D // dt
    conv_input = z3 is None
    slab = lambda k: (lambda d, b: (b0 + b, 0, k * n_d + d))
    cols = lambda k: (lambda d, b: (0, k * n_d + d))
    seq_block = (1, L, dt)
    own = lambda d, b: (b, 0, d)
    in_specs, args = [], []
    if conv_input:
        in_specs += [pl.BlockSpec(seq_block, slab(0)), pl.BlockSpec((SHORT_CONV, dt), cols(0)),
                     pl.BlockSpec((1, dt), cols(0))]
        args += [u3, conv_w, conv_b]
    else:
        in_specs += [pl.BlockSpec(seq_block, own)]
        args += [z3]
    in_specs += [pl.BlockSpec(seq_block, slab(order + 1)), pl.BlockSpec((SHORT_CONV, dt), cols(order + 1)),
                 pl.BlockSpec((1, dt), cols(order + 1)),
                 pl.BlockSpec((1, dt), lambda d, b: (0, d)),
                 pl.BlockSpec((1, 2 * nb - 1, 2 * nfp, dt), lambda d, b: (order, 0, 0, d),
                              pipeline_mode=pl.Buffered(1)),
                 pl.BlockSpec((2 * nfp, bk), lambda d, b: (0, 0)),
                 pl.BlockSpec((bk, 2 * nfp), lambda d, b: (0, 0))]
    args += [u3, conv_w, conv_b, skip[order:order + 1], spectra, fwd, inv]
    return pl.pallas_call(
        functools.partial(_long_conv_kernel, nb=nb, bk=bk, nfp=nfp, conv_input=conv_input),
        grid=(n_d, nbatch),
        in_specs=in_specs,
        out_specs=pl.BlockSpec(seq_block, own),
        out_shape=jax.ShapeDtypeStruct((nbatch, L, D), out_dtype),
        scratch_shapes=[pltpu.VMEM((nb, 2 * nfp, dt), F32), pltpu.VMEM((2 * nfp, dt), F32)],
        compiler_params=_params(("arbitrary", "arbitrary")),
        name="hyena_long_conv",
    )(*args)


def _hyena_mixer(x, groups, p):
    T, D = x.shape
    u = _project(x, p["hy_w_in"].astype(BF16), p["hy_b_in"], F32)
    conv_w, conv_b, skip = p["hy_conv_w"], p["hy_conv_b"].reshape(1, 3 * D), p["hy_filt_skip"]
    outs = []
    for first, nbatch, L in groups:
        bk = min(L, HYENA_BLOCK)
        dt = _tile(D, 256 if L <= 2048 else 128)
        fwd, inv, nfp = _dft_matrices(bk)
        kappa, nrm = _hyena_filters(L, D, p["hy_filt_w_in"], p["hy_filt_b_in"], p["hy_filt_w_mid"],
                                    p["hy_filt_b_mid"], p["hy_filt_freq"], p["hy_filt_w_out"])
        spectra = _hyena_spectra(kappa, nrm, fwd, bk, nfp)
        kw = dict(b0=first // L, nbatch=nbatch, bk=bk, nfp=nfp, dt=dt)
        u3 = u.reshape(T // L, L, 3 * D)
        z1 = _long_conv(u3, None, 0, spectra, fwd, inv, conv_w, conv_b, skip, out_dtype=F32, **kw)
        z2 = _long_conv(u3, z1, 1, spectra, fwd, inv, conv_w, conv_b, skip, out_dtype=BF16, **kw)
        outs.append(z2.reshape(nbatch * L, D))
    return jnp.concatenate(outs, axis=0)


def _attn_kernel(q_ref, k_ref, v_ref, lam_ref, g_ref, o_ref, *, lam_init, kc):
    dh = DIFF_HEAD_DIM
    lam_dots = jnp.sum(lam_ref[0:2, :] * lam_ref[2:4, :], axis=-1, keepdims=True)
    lam = jnp.exp(lam_dots[0:1, :]) - jnp.exp(lam_dots[1:2, :]) + lam_init
    q = q_ref[...]
    n_chunks = k_ref.shape[0] // kc
    state = [None, None]
    for j in range(n_chunks):
        rows = slice(j * kc, (j + 1) * kc)
        v = v_ref[rows, :]
        for c in range(2):
            s = lax.dot_general(q[:, c * dh:(c + 1) * dh], k_ref[rows, c * dh:(c + 1) * dh],
                                (((1,), (1,)), ((), ())), preferred_element_type=F32)
            top = jnp.max(s, axis=-1, keepdims=True)
            if state[c] is None:
                e = jnp.exp2(s - top)
                state[c] = (top, jnp.sum(e, axis=-1, keepdims=True),
                            jnp.dot(e.astype(BF16), v, preferred_element_type=F32))
            else:
                m_old, l_old, acc = state[c]
                m_new = jnp.maximum(m_old, top)
                shrink = jnp.exp2(m_old - m_new)
                e = jnp.exp2(s - m_new)
                state[c] = (m_new, shrink * l_old + jnp.sum(e, axis=-1, keepdims=True),
                            shrink * acc + jnp.dot(e.astype(BF16), v, preferred_element_type=F32))
    (_, l0, o0), (_, l1, o1) = state
    o = o0 * (1.0 / l0) - o1 * (lam / l1)
    o = o * lax.rsqrt(jnp.mean(o * o, axis=-1, keepdims=True) + LN_EPS) * g_ref[...]
    o_ref[...] = (o * (1.0 - lam_init)).astype(o_ref.dtype)


def _diff_attention(qkv, lam_vecs, subln_g, layer_idx, *, first, nbatch, L, tq, kc):
    T, three_d = qkv.shape
    D = three_d // 3
    hw = 2 * DIFF_HEAD_DIM
    H = D // hw
    tq, kc = _tile(L, tq), _tile(L, kc)
    nq = L // tq
    q0, b0 = first // tq, first // L
    lam_init = 0.8 - 0.6 * math.exp(-0.3 * layer_idx)
    return pl.pallas_call(
        functools.partial(_attn_kernel, lam_init=lam_init, kc=kc),
        grid=(nbatch, H, nq),
        in_specs=[pl.BlockSpec((tq, hw), lambda b, h, i: (q0 + b * nq + i, h)),
                  pl.BlockSpec((L, hw), lambda b, h, i: (b0 + b, H + h)),
                  pl.BlockSpec((L, hw), lambda b, h, i: (b0 + b, 2 * H + h)),
                  pl.BlockSpec((4, DIFF_HEAD_DIM), lambda b, h, i: (0, 0)),
                  pl.BlockSpec((1, hw), lambda b, h, i: (0, 0))],
        out_specs=pl.BlockSpec((tq, hw), lambda b, h, i: (b * nq + i, h)),
        out_shape=jax.ShapeDtypeStruct((nbatch * L, D), BF16),
        compiler_params=_params(("arbitrary", "arbitrary", "arbitrary")),
        name="diff_attention",
    )(qkv, qkv, qkv, lam_vecs, subln_g.reshape(1, hw))


def _attention_mixer(x, groups, p, layer_idx):
    split = groups[1][0] if len(groups) > 1 else x.shape[0]
    l1, l2 = groups[0][2], groups[-1][2]
    qkv = _qkv_project(x, p["da_w_qkv"].astype(BF16), split, l1, l2)
    lam_vecs = jnp.stack([p["da_lambda_q1"], p["da_lambda_q2"], p["da_lambda_k1"], p["da_lambda_k2"]])
    outs = [_diff_attention(qkv, lam_vecs, p["da_subln_g"], layer_idx, first=first, nbatch=nbatch, L=L,
                            tq=ATTN_Q_TILE * (1 if L > 2048 else 2), kc=ATTN_KEY_CHUNK // (1 if L > 2048 else 2))
            for first, nbatch, L in groups]
    return jnp.concatenate(outs, axis=0)


def _router_kernel(x_ref, w_ref, b_ref, idx_ref, gate_ref, rank_ref, cnt_ref, base_ref, *, n_experts):
    i = pl.program_id(0)

    @pl.when(i == 0)
    def _():
        base_ref[...] = jnp.zeros_like(base_ref)

    logits = _dot3(x_ref[...], w_ref[...]) + b_ref[...]
    tm = logits.shape[0]
    lane = lax.broadcasted_iota(jnp.int32, (tm, LANES), 1)
    neg_inf = jnp.float32(-jnp.inf)
    lane_f = lane.astype(F32)
    live = jnp.where(lane < n_experts, logits, neg_inf)
    vals, picks, hots = [], [], []
    for _ in range(TOP_K):
        best = jnp.max(live, axis=-1, keepdims=True)
        pick = jnp.min(jnp.where(live == best, lane_f, float(LANES)), axis=-1, keepdims=True)
        hot = lane_f == pick
        live = jnp.where(hot, neg_inf, live)
        vals.append(best)
        picks.append(pick.astype(jnp.int32))
        hots.append(hot)
    exps = [jnp.exp(v - vals[0]) for v in vals]
    denom = exps[0] + exps[1] + exps[2] + exps[3]
    chosen = jnp.zeros((tm, LANES), F32)
    for hot in hots:
        chosen = chosen + hot.astype(F32)
    r_i = lax.broadcasted_iota(jnp.int32, (tm, tm), 0)
    c_i = lax.broadcasted_iota(jnp.int32, (tm, tm), 1)
    earlier = jnp.where(r_i > c_i, 1.0, 0.0).astype(BF16)
    before = base_ref[0:1, :] + jnp.dot(earlier, chosen.astype(BF16), preferred_element_type=F32)
    idx_out = jnp.zeros((tm, LANES), jnp.int32)
    rank_out = jnp.zeros((tm, LANES), jnp.int32)
    gate_out = jnp.zeros((tm, LANES), F32)
    for k in range(TOP_K):
        rank_k = jnp.sum(jnp.where(hots[k], before, 0.0), axis=-1, keepdims=True).astype(jnp.int32)
        idx_out = jnp.where(lane == k, picks[k], idx_out)
        rank_out = jnp.where(lane == k, rank_k, rank_out)
        gate_out = jnp.where(lane == k, exps[k] / denom, gate_out)
    idx_ref[...] = idx_out
    rank_ref[...] = rank_out
    gate_ref[...] = gate_out
    base_ref[...] = base_ref[...] + jnp.sum(chosen, axis=0, keepdims=True)
    cnt_ref[...] = base_ref[...]


def _route(x, router_w, router_b, tm=512):
    T, D = x.shape
    E = router_w.shape[1]
    tm = _tile(T, tm)
    w = jnp.pad(router_w, ((0, 0), (0, LANES - E)))
    b = jnp.pad(router_b, (0, LANES - E)).reshape(1, LANES)
    row = lambda i: (i, 0)
    fixed = lambda i: (0, 0)
    wide = pl.BlockSpec((tm, LANES), row)
    return pl.pallas_call(
        functools.partial(_router_kernel, n_experts=E),
        grid=(T // tm,),
        in_specs=[pl.BlockSpec((tm, D), row), pl.BlockSpec((D, LANES), fixed), pl.BlockSpec((1, LANES), fixed)],
        out_specs=[wide, wide, wide, pl.BlockSpec((SUBLANES, LANES), fixed)],
        out_shape=[jax.ShapeDtypeStruct((T, LANES), jnp.int32), jax.ShapeDtypeStruct((T, LANES), F32),
                   jax.ShapeDtypeStruct((T, LANES), jnp.int32), jax.ShapeDtypeStruct((SUBLANES, LANES), F32)],
        scratch_shapes=[pltpu.VMEM((SUBLANES, LANES), F32)],
        compiler_params=_params(("arbitrary",)),
        name="moe_route",
    )(x, w, b)


def _row_copy(src, src_row, dst, dst_row, sem):
    return pltpu.make_async_copy(src.at[pl.ds(src_row, 1)], dst.at[pl.ds(dst_row, 1)], sem)


def _dispatch_kernel(pend_ref, padded_ref, dest_ref, x_ref, xs_out, zeros, sem, zero_sem, *, tm, rb, n_experts):
    @pl.when(pl.program_id(0) == 0)
    def _():
        zeros[...] = jnp.zeros_like(zeros)

        def last_tile(e):
            first_row = pl.multiple_of(pend_ref[e] - rb, rb)
            return pltpu.make_async_copy(zeros, xs_out.at[pl.ds(first_row, rb)], zero_sem)

        for e in range(n_experts):
            @pl.when(padded_ref[e] > 0)
            def _(e=e):
                last_tile(e).start()

        for e in range(n_experts):
            @pl.when(padded_ref[e] > 0)
            def _(e=e):
                last_tile(e).wait()

    def issue(group, carry):
        base = pl.multiple_of(group * SUBLANES, SUBLANES)
        for u in range(SUBLANES):
            for k in range(TOP_K):
                _row_copy(x_ref, base + u, xs_out, dest_ref[(base + u) * TOP_K + k], sem).start(priority=k % 2)
        return carry

    lax.fori_loop(0, tm // SUBLANES, issue, 0)
    done = xs_out.at[pl.ds(0, TOP_K * tm)]
    pltpu.make_async_copy(done, done, sem).wait()


def _dispatch(x_packed, dest_flat, padded_end, padded, n_rows, rb, tm=256):
    T, half = x_packed.shape
    tm = _tile(T, tm)
    return pl.pallas_call(
        functools.partial(_dispatch_kernel, tm=tm, rb=rb, n_experts=padded.shape[0]),
        grid_spec=pltpu.PrefetchScalarGridSpec(
            num_scalar_prefetch=2,
            grid=(T // tm,),
            in_specs=[pl.BlockSpec((tm * TOP_K,), lambda i, pe, pd: (i,), memory_space=pltpu.SMEM),
                      pl.BlockSpec((tm, half), lambda i, pe, pd: (i, 0))],
            out_specs=pl.BlockSpec(memory_space=pl.ANY),
            scratch_shapes=[pltpu.VMEM((rb, half), jnp.uint32), pltpu.SemaphoreType.DMA(()),
                            pltpu.SemaphoreType.DMA(())]),
        out_shape=jax.ShapeDtypeStruct((n_rows, half), jnp.uint32),
        compiler_params=_params(("arbitrary",)),
        name="moe_dispatch",
    )(padded_end, padded, dest_flat, x_packed)


def _expert_kernel(tile_expert_ref, n_used_ref, x_ref, bg_ref, bu_ref, bd_ref, wg_hbm, wu_hbm, wd_hbm, o_ref,
                   wg, wu, wd, stage, sem, *, chunk):
    i = pl.program_id(0)
    expert = tile_expert_ref[i]

    @pl.when(i < n_used_ref[0])
    def _():
        @pl.when((i == 0) | (tile_expert_ref[jnp.maximum(i - 1, 0)] != expert))
        def _():
            pieces = [(src, dst, r) for src, dst in ((wg_hbm, wg), (wu_hbm, wu), (wd_hbm, wd))
                      for r in range(0, dst.shape[0], chunk)]

            def fetch(n):
                src, _, r = pieces[n]
                return pltpu.make_async_copy(src.at[expert, pl.ds(r, chunk)], stage.at[n % 2], sem.at[n % 2])

            fetch(0).start()
            for n, (_, dst, r) in enumerate(pieces):
                if n + 1 < len(pieces):
                    fetch(n + 1).start()
                fetch(n).wait()
                dst[r:r + chunk, :] = stage[n % 2].astype(BF16)

        x = _unpack_bf16_pairs(x_ref[...])
        gate = jnp.dot(x, wg[...], preferred_element_type=F32) + bg_ref[0]
        up = jnp.dot(x, wu[...], preferred_element_type=F32) + bu_ref[0]
        gate = jnp.minimum(gate, SWIGLU_LIMIT)
        up = jnp.clip(up, -SWIGLU_LIMIT, SWIGLU_LIMIT)
        h = gate * (1.0 / (1.0 + jnp.exp(-SWIGLU_ALPHA * gate))) * (up + 1.0)
        y = jnp.dot(h.astype(BF16), wd[...], preferred_element_type=F32) + bd_ref[0]
        o_ref[...] = _pack_bf16_pairs(y)


def _experts(xs, tile_expert, n_used, wg, bg, wu, bu, wd, bd, rb, chunk=512):
    P, half = xs.shape
    E, D, F = wg.shape
    assert D == F, "one staging buffer shape serves all three weight matrices"
    n_tiles = P // rb
    chunk = _tile(D, chunk)
    of_expert = lambda i, te, nu: (te[i], 0, 0)
    hbm = pl.BlockSpec(memory_space=pl.ANY)
    return pl.pallas_call(
        functools.partial(_expert_kernel, chunk=chunk),
        grid_spec=pltpu.PrefetchScalarGridSpec(
            num_scalar_prefetch=2,
            grid=(n_tiles,),
            in_specs=[pl.BlockSpec((rb, half), lambda i, te, nu: (jnp.minimum(i, nu[0] - 1), 0)),
                      pl.BlockSpec((1, 1, F), of_expert), pl.BlockSpec((1, 1, F), of_expert),
                      pl.BlockSpec((1, 1, D), of_expert), hbm, hbm, hbm],
            out_specs=pl.BlockSpec((rb, half), lambda i, te, nu: (i, 0)),
            scratch_shapes=[pltpu.VMEM((D, F), BF16), pltpu.VMEM((D, F), BF16), pltpu.VMEM((F, D), BF16),
                            pltpu.VMEM((2, chunk, F), F32), pltpu.SemaphoreType.DMA((2,))]),
        out_shape=jax.ShapeDtypeStruct((P, half), jnp.uint32),
        compiler_params=_params(("arbitrary",)),
        name="moe_experts",
    )(tile_expert, n_used, xs, bg, bu, bd, wg, wu, wd)


def _combine_kernel(dest_ref, dest_next_ref, gate_ref, x_ref, g_ref, beta_ref, ys_hbm, *rest,
                    tc, n_steps, n_first):
    *o_refs, rows, sem = rest
    i = pl.program_id(0)
    slot = lax.rem(i, 2)

    def gather(d_ref, s):
        def issue(group, carry):
            base = pl.multiple_of(group * SUBLANES, SUBLANES)
            for u in range(SUBLANES):
                for k in range(TOP_K):
                    _row_copy(ys_hbm, d_ref[(base + u) * TOP_K + k], rows.at[s, k], base + u,
                              sem.at[s]).start(priority=k % 2)
            return carry
        lax.fori_loop(0, tc // SUBLANES, issue, 0)

    @pl.when(i == 0)
    def _():
        gather(dest_ref, 0)

    @pl.when(i + 1 < n_steps)
    def _():
        gather(dest_next_ref, 1 - slot)

    pltpu.make_async_copy(rows.at[slot], rows.at[slot], sem.at[slot]).wait()
    gates = gate_ref[...]
    mix_hi = mix_lo = None
    for k in range(TOP_K):
        w = rows[slot, k]
        hi = gates[:, k:k + 1] * pltpu.bitcast(w & jnp.uint32(0xFFFF0000), F32)
        lo = gates[:, k:k + 1] * pltpu.bitcast(w << 16, F32)
        mix_hi = hi if mix_hi is None else mix_hi + hi
        mix_lo = lo if mix_lo is None else mix_lo + lo
    mix = jnp.concatenate([mix_hi, mix_lo], axis=1)
    y = _layer_norm(ALPHA * x_ref[...] + mix, g_ref[...], beta_ref[...])
    if n_first is None:
        o_refs[0][...] = y
    else:
        @pl.when(i < n_first)
        def _():
            o_refs[0][...] = y

        @pl.when(i >= n_first)
        def _():
            o_refs[1][...] = y


def _combine_ln(ys, dest_flat, gates, x, g, beta, split=None, tc=128):
    T, D = x.shape
    tc = _tile(T if split is None else math.gcd(split, T - split), tc)
    n_steps = T // tc
    row = lambda i: (i, 0)
    fixed = lambda i: (0, 0)
    tile = (tc, D)
    if split is None:
        n_first = None
        out_specs = pl.BlockSpec(tile, row)
        out_shape = jax.ShapeDtypeStruct((T, D), F32)
    else:
        n_first = split // tc
        out_specs = [pl.BlockSpec(tile, lambda i: (jnp.minimum(i, n_first - 1), 0)),
                     pl.BlockSpec(tile, lambda i: (jnp.maximum(i - n_first, 0), 0))]
        out_shape = [jax.ShapeDtypeStruct((split, D), F32), jax.ShapeDtypeStruct((T - split, D), F32)]
    return pl.pallas_call(
        functools.partial(_combine_kernel, tc=tc, n_steps=n_steps, n_first=n_first),
        grid=(n_steps,),
        in_specs=[pl.BlockSpec((tc * TOP_K,), lambda i: (i,), memory_space=pltpu.SMEM),
                  pl.BlockSpec((tc * TOP_K,), lambda i: (jnp.minimum(i + 1, n_steps - 1),),
                               memory_space=pltpu.SMEM),
                  pl.BlockSpec((tc, LANES), row), pl.BlockSpec(tile, row),
                  pl.BlockSpec((1, D), fixed), pl.BlockSpec((1, D), fixed),
                  pl.BlockSpec(memory_space=pl.ANY)],
        out_specs=out_specs,
        out_shape=out_shape,
        scratch_shapes=[pltpu.VMEM((2, TOP_K, tc, D // 2), jnp.uint32), pltpu.SemaphoreType.DMA((2,))],
        compiler_params=_params(("arbitrary",)),
        name="moe_combine_ln",
    )(dest_flat, dest_flat, gates, x, g.reshape(1, D), beta.reshape(1, D), ys)


def _expert_stacks(p):
    n_layers, E, D, F = p["moe_w_gate"].shape
    stack = lambda w: w.reshape((n_layers * E,) + w.shape[2:])
    bias = lambda b: b.reshape(n_layers * E, 1, b.shape[-1])
    return (stack(p["moe_w_gate"]), bias(p["moe_b_gate"]), stack(p["moe_w_up"]), bias(p["moe_b_up"]),
            stack(p["moe_w_down"]), bias(p["moe_b_down"]))


def _moe_ln(x, x_packed, p, stacks, i, rb, split=None):
    T, D = x.shape
    E = p["moe_router_w"].shape[-1]
    idx, gates, rank, counts = _route(x, p["moe_router_w"][i], p["moe_router_b"][i])
    counts = counts[0, :E].astype(jnp.int32)
    padded = (counts + rb - 1) // rb * rb
    padded_end = jnp.cumsum(padded)
    padded_start = padded_end - padded
    n_tiles = (T * TOP_K) // rb + E
    tile_expert = jnp.minimum(jnp.searchsorted(padded_end, jnp.arange(n_tiles, dtype=jnp.int32) * rb, side="right"),
                              E - 1).astype(jnp.int32)
    n_used = (padded_end[-1:] // rb).astype(jnp.int32)
    dest = (padded_start[idx[:, :TOP_K]] + rank[:, :TOP_K]).reshape(T * TOP_K)
    xs = _dispatch(x_packed, dest, padded_end.astype(jnp.int32), padded, n_tiles * rb, rb)
    ys = _experts(xs, tile_expert + i * E, n_used, *stacks, rb)
    return _combine_ln(ys, dest, gates, x, p["ln2_g"][i], p["ln2_b"][i], split)


def _encoder(x, groups, p, rb):
    stacks = _expert_stacks(p)
    for i in range(DEPTH):
        j = i // 2
        if i % 2 == 0:
            pj = {k: v[j] for k, v in p.items() if k.startswith("hy_")}
            a = _hyena_mixer(x, groups, pj)
            w_o, b_o = pj["hy_w_out"], pj["hy_b_out"]
        else:
            pj = {k: v[j] for k, v in p.items() if k.startswith("da_")}
            a = _attention_mixer(x, groups, pj, i)
            w_o, b_o = pj["da_w_o"], jnp.zeros((x.shape[1],), F32)
        x, x_packed = _out_project_ln(a, w_o.astype(BF16), b_o, x, p["ln1_g"][i], p["ln1_b"][i])
        last = i == DEPTH - 1
        x = _moe_ln(x, x_packed, p, stacks, i, rb, groups[1][0] if last and len(groups) > 1 else None)
    return x


def _forward(x_prompt, x_sample, p, rb=512):
    B1, L1, D = x_prompt.shape
    B2, L2, _ = x_sample.shape
    T1, T2 = B1 * L1, B2 * L2
    assert T1 % L2 == 0, "sample sequences must start on a multiple of their length"
    x = jnp.concatenate([x_prompt.reshape(T1, D), x_sample.reshape(T2, D)], axis=0)
    y1, y2 = _encoder(x, [(0, B1, L1), (T1, B2, L2)], p, rb)
    return y1.reshape(B1, L1, D), y2.reshape(B2, L2, D)


def kernel(x_prompt, x_sample, hy_w_in, hy_b_in, hy_conv_w, hy_conv_b, hy_filt_w_in, hy_filt_b_in, hy_filt_w_mid, hy_filt_b_mid, hy_filt_freq, hy_filt_w_out, hy_filt_skip, hy_w_out, hy_b_out, da_w_qkv, da_lambda_q1, da_lambda_k1, da_lambda_q2, da_lambda_k2, da_subln_g, da_w_o, ln1_g, ln1_b, moe_router_w, moe_router_b, moe_w_gate, moe_b_gate, moe_w_up, moe_b_up, moe_w_down, moe_b_down, ln2_g, ln2_b):
    p = dict(hy_w_in=hy_w_in, hy_b_in=hy_b_in, hy_conv_w=hy_conv_w, hy_conv_b=hy_conv_b,
             hy_filt_w_in=hy_filt_w_in, hy_filt_b_in=hy_filt_b_in, hy_filt_w_mid=hy_filt_w_mid,
             hy_filt_b_mid=hy_filt_b_mid, hy_filt_freq=hy_filt_freq, hy_filt_w_out=hy_filt_w_out,
             hy_filt_skip=hy_filt_skip, hy_w_out=hy_w_out, hy_b_out=hy_b_out,
             da_w_qkv=da_w_qkv, da_lambda_q1=da_lambda_q1, da_lambda_k1=da_lambda_k1,
             da_lambda_q2=da_lambda_q2, da_lambda_k2=da_lambda_k2, da_subln_g=da_subln_g, da_w_o=da_w_o,
             ln1_g=ln1_g, ln1_b=ln1_b, moe_router_w=moe_router_w, moe_router_b=moe_router_b,
             moe_w_gate=moe_w_gate, moe_b_gate=moe_b_gate, moe_w_up=moe_w_up, moe_b_up=moe_b_up,
             moe_w_down=moe_w_down, moe_b_down=moe_b_down, ln2_g=ln2_g, ln2_b=ln2_b)
    return _forward(x_prompt, x_sample, p)
```

```python
import functools
import math

import jax
import jax.numpy as jnp
from jax import lax
from jax.experimental import pallas as pl
from jax.experimental.pallas import tpu as pltpu

DEPTH = 2
SHORT_CONV = 3
HYENA_ORDER = 2
FILTER_EMB = 33
FILTER_BANDS = (FILTER_EMB - 1) // 2
FILTER_INNER = 2
FAST_DECAY_PCT = 0.3
SLOW_DECAY_PCT = 1.5
DECAY_TARGET = 1e-2
MAX_DECAY = math.log(DECAY_TARGET) / FAST_DECAY_PCT
MIN_DECAY = math.log(DECAY_TARGET) / SLOW_DECAY_PCT
DIFF_HEAD_DIM = 128
ROPE_DIM = DIFF_HEAD_DIM // 4
ROPE_THETA = 500000.0
TOP_K = 4
SWIGLU_ALPHA = 1.702
SWIGLU_LIMIT = 7.0
LN_EPS = 1e-5
ALPHA = (2 * DEPTH) ** 0.25

LANES = 128
SUBLANES = 8
VMEM_LIMIT_BYTES = 60 * 1024 * 1024

HYENA_BLOCK = 512

ATTN_Q_TILE = 512
ATTN_KEY_CHUNK = 2048

F32 = jnp.float32
BF16 = jnp.bfloat16


def _params(semantics, vmem=VMEM_LIMIT_BYTES):
    return pltpu.CompilerParams(dimension_semantics=semantics, vmem_limit_bytes=vmem)


def _tile(n, pref):
    t = min(n, pref)
    assert n % t == 0, (n, t)
    return t


def _split_bf16(a):
    hi = a.astype(BF16)
    lo = (a - hi.astype(F32)).astype(BF16)
    return hi, lo


def _dot3(a, b):
    ah, al = _split_bf16(a)
    bh, bl = _split_bf16(b)
    d = functools.partial(jnp.dot, preferred_element_type=F32)
    return d(ah, bh) + d(al, bh) + d(ah, bl)


def _layer_norm(y, g, b):
    mu = jnp.mean(y, axis=-1, keepdims=True)
    yc = y - mu
    var = jnp.mean(yc * yc, axis=-1, keepdims=True)
    return yc * lax.rsqrt(var + LN_EPS) * g + b


def _row_sources(arrays, tm):
    specs, starts, lo = [], [], 0
    for arr in arrays:
        n = arr.shape[0] // tm
        assert n * tm == arr.shape[0], (arr.shape, tm)
        specs.append(pl.BlockSpec((tm, arr.shape[1]), lambda *idx, lo=lo, n=n: (jnp.clip(idx[-1] - lo, 0, n - 1), 0)))
        starts.append(lo)
        lo += n
    return specs, tuple(starts)


def _select_rows(refs, starts, i):
    rows = refs[0][...]
    for ref, lo in zip(refs[1:], starts[1:]):
        rows = jnp.where(i >= lo, ref[...], rows)
    return rows


def _proj_kernel(*refs, starts):
    *x_refs, w_ref, b_ref, o_ref = refs
    x = _select_rows(x_refs, starts, pl.program_id(1))
    acc = jnp.dot(x.astype(BF16), w_ref[...], preferred_element_type=F32)
    o_ref[...] = (acc + b_ref[...]).astype(o_ref.dtype)


def _project(xs, w, b, out_dtype, tm=512, tn=2048):
    T = sum(x.shape[0] for x in xs)
    K, N = w.shape
    tm, tn = _tile(math.gcd(*[x.shape[0] for x in xs]), tm), _tile(N, tn)
    x_specs, starts = _row_sources(xs, tm)
    return pl.pallas_call(
        functools.partial(_proj_kernel, starts=starts),
        grid=(N // tn, T // tm),
        in_specs=x_specs + [pl.BlockSpec((K, tn), lambda n, m: (0, n)),
                            pl.BlockSpec((1, tn), lambda n, m: (0, n))],
        out_specs=pl.BlockSpec((tm, tn), lambda n, m: (m, n)),
        out_shape=jax.ShapeDtypeStruct((T, N), out_dtype),
        compiler_params=_params(("arbitrary", "arbitrary")),
        name="project",
    )(*xs, w, b.reshape(1, N))


def _qkv_kernel(x_ref, w_ref, rope_ref, o_ref, *, n_rope_slabs):
    n = pl.program_id(0)
    acc = jnp.dot(x_ref[...].astype(BF16), w_ref[...], preferred_element_type=F32)
    tn = acc.shape[1]

    @pl.when(n >= n_rope_slabs)
    def _():
        o_ref[...] = acc.astype(o_ref.dtype)

    @pl.when(n < n_rope_slabs)
    def _():
        scale = jnp.where(n == 0, DIFF_HEAD_DIM ** -0.5 * math.log2(math.e), 1.0)
        c_tab, s_lo, s_hi = rope_ref[0] * scale, rope_ref[1] * scale, rope_ref[2] * scale
        half = ROPE_DIM // 2
        for h in range(tn // LANES):
            xh = acc[:, h * LANES:(h + 1) * LANES]
            rot = (xh * c_tab + pltpu.roll(xh, LANES - half, 1) * s_lo
                   + pltpu.roll(xh, half, 1) * s_hi)
            o_ref[:, h * LANES:(h + 1) * LANES] = rot.astype(o_ref.dtype)


def _rope_tables(n_pos):
    half = ROPE_DIM // 2
    inv_freq = ROPE_THETA ** (-jnp.arange(0, ROPE_DIM, 2, dtype=F32) / ROPE_DIM)
    ang = jnp.arange(n_pos, dtype=F32)[:, None] * inv_freq[None, :]
    cos, sin = jnp.cos(ang), jnp.sin(ang)
    rest = LANES - ROPE_DIM
    pad = lambda parts, fill: jnp.concatenate(parts + [jnp.full((n_pos, rest), fill, F32)], axis=1)
    zero = jnp.zeros_like(sin)
    return jnp.stack([pad([cos, cos], 1.0), pad([-sin, zero], 0.0), pad([zero, sin], 0.0)])


def _qkv_project(x, w, split, l1, l2, tm=512):
    T, K = x.shape
    N = w.shape[1]
    tn = N // 3
    tm = _tile(math.gcd(l1, l2), tm)

    def pos_block(m):
        start = m * tm
        pos0 = jnp.where(start < split, lax.rem(start, l1), lax.rem(jnp.maximum(start - split, 0), l2))
        return pos0 // tm

    return pl.pallas_call(
        functools.partial(_qkv_kernel, n_rope_slabs=2),
        grid=(3, T // tm),
        in_specs=[pl.BlockSpec((tm, K), lambda n, m: (m, 0)),
                  pl.BlockSpec((K, tn), lambda n, m: (0, n)),
                  pl.BlockSpec((3, tm, LANES), lambda n, m: (0, pos_block(m), 0))],
        out_specs=pl.BlockSpec((tm, tn), lambda n, m: (m, n)),
        out_shape=jax.ShapeDtypeStruct((T, N), BF16),
        compiler_params=_params(("arbitrary", "arbitrary")),
        name="qkv_project",
    )(x, w, _rope_tables(max(l1, l2)))


def _pack_bf16_pairs(y):
    half = y.shape[1] // 2
    r = pltpu.bitcast(y.astype(BF16).astype(F32), jnp.uint32)
    return (r[:, :half] & jnp.uint32(0xFFFF0000)) | (r[:, half:] >> 16)


def _unpack_bf16_pairs(w):
    hi = pltpu.bitcast(w & jnp.uint32(0xFFFF0000), F32)
    lo = pltpu.bitcast(w << 16, F32)
    return jnp.concatenate([hi, lo], axis=1).astype(BF16)


def _route_rows(x, w_ref, b_ref, idx_ref, gate_ref, rank_ref, cnt_ref, base_ref, n_experts):
    @pl.when(pl.program_id(0) == 0)
    def _():
        base_ref[...] = jnp.zeros_like(base_ref)

    logits = _dot3(x, w_ref[...]) + b_ref[...]
    tm = logits.shape[0]
    lane = lax.broadcasted_iota(jnp.int32, (tm, LANES), 1)
    neg_inf = jnp.float32(-jnp.inf)
    lane_f = lane.astype(F32)
    live = jnp.where(lane < n_experts, logits, neg_inf)
    vals, picks, hots = [], [], []
    for _ in range(TOP_K):
        best = jnp.max(live, axis=-1, keepdims=True)
        pick = jnp.min(jnp.where(live == best, lane_f, float(LANES)), axis=-1, keepdims=True)
        hot = lane_f == pick
        live = jnp.where(hot, neg_inf, live)
        vals.append(best)
        picks.append(pick.astype(jnp.int32))
        hots.append(hot)
    exps = [jnp.exp(v - vals[0]) for v in vals]
    denom = exps[0] + exps[1] + exps[2] + exps[3]
    chosen = jnp.zeros((tm, LANES), F32)
    for hot in hots:
        chosen = chosen + hot.astype(F32)
    r_i = lax.broadcasted_iota(jnp.int32, (tm, tm), 0)
    c_i = lax.broadcasted_iota(jnp.int32, (tm, tm), 1)
    earlier = jnp.where(r_i > c_i, 1.0, 0.0).astype(BF16)
    before = base_ref[0:1, :] + jnp.dot(earlier, chosen.astype(BF16), preferred_element_type=F32)
    idx_out = jnp.zeros((tm, LANES), jnp.int32)
    rank_out = jnp.zeros((tm, LANES), jnp.int32)
    gate_out = jnp.zeros((tm, LANES), F32)
    for k in range(TOP_K):
        rank_k = jnp.sum(jnp.where(hots[k], before, 0.0), axis=-1, keepdims=True).astype(jnp.int32)
        idx_out = jnp.where(lane == k, picks[k], idx_out)
        rank_out = jnp.where(lane == k, rank_k, rank_out)
        gate_out = jnp.where(lane == k, exps[k] / denom, gate_out)
    idx_ref[...] = idx_out
    rank_ref[...] = rank_out
    gate_ref[...] = gate_out
    base_ref[...] = base_ref[...] + jnp.sum(chosen, axis=0, keepdims=True)
    cnt_ref[...] = base_ref[...]


def _out_ln_route_kernel(*refs, n_a, n_x, a_starts, x_starts, n_experts):
    a_refs, refs = refs[:n_a], refs[n_a:]
    (w_ref, b_ref), refs = refs[:2], refs[2:]
    x_refs, refs = refs[:n_x], refs[n_x:]
    g_ref, beta_ref, rw_ref, rb_ref, o_ref, p_ref, idx_ref, gate_ref, rank_ref, cnt_ref, base_ref = refs
    i = pl.program_id(0)
    h = jnp.dot(_select_rows(a_refs, a_starts, i), w_ref[...], preferred_element_type=F32) + b_ref[...]
    y = _layer_norm(ALPHA * _select_rows(x_refs, x_starts, i) + h, g_ref[...], beta_ref[...])
    o_ref[...] = y
    p_ref[...] = _pack_bf16_pairs(y)
    _route_rows(y, rw_ref, rb_ref, idx_ref, gate_ref, rank_ref, cnt_ref, base_ref, n_experts)


def _out_project_ln_route(a_parts, w, b, x_parts, g, beta, router_w, router_b, tm=512):
    T = sum(a.shape[0] for a in a_parts)
    K, D = w.shape
    E = router_w.shape[1]
    tm = _tile(math.gcd(*[v.shape[0] for v in a_parts + x_parts]), tm)
    a_specs, a_starts = _row_sources(a_parts, tm)
    x_specs, x_starts = _row_sources(x_parts, tm)
    row = lambda i: (i, 0)
    fixed = lambda i: (0, 0)
    once = dict(pipeline_mode=pl.Buffered(1))
    wide = pl.BlockSpec((tm, LANES), row)
    rw = jnp.pad(router_w, ((0, 0), (0, LANES - E)))
    rb = jnp.pad(router_b, (0, LANES - E)).reshape(1, LANES)
    return pl.pallas_call(
        functools.partial(_out_ln_route_kernel, n_a=len(a_parts), n_x=len(x_parts), a_starts=a_starts,
                          x_starts=x_starts, n_experts=E),
        grid=(T // tm,),
        in_specs=(a_specs + [pl.BlockSpec((K, D), fixed, **once), pl.BlockSpec((1, D), fixed)] + x_specs
                  + [pl.BlockSpec((1, D), fixed), pl.BlockSpec((1, D), fixed),
                     pl.BlockSpec((D, LANES), fixed, **once), pl.BlockSpec((1, LANES), fixed)]),
        out_specs=[pl.BlockSpec((tm, D), row), pl.BlockSpec((tm, D // 2), row), wide, wide, wide,
                   pl.BlockSpec((SUBLANES, LANES), fixed)],
        out_shape=[jax.ShapeDtypeStruct((T, D), F32), jax.ShapeDtypeStruct((T, D // 2), jnp.uint32),
                   jax.ShapeDtypeStruct((T, LANES), jnp.int32), jax.ShapeDtypeStruct((T, LANES), F32),
                   jax.ShapeDtypeStruct((T, LANES), jnp.int32), jax.ShapeDtypeStruct((SUBLANES, LANES), F32)],
        scratch_shapes=[pltpu.VMEM((SUBLANES, LANES), F32)],
        compiler_params=_params(("arbitrary",)),
        name="out_project_ln_route",
    )(*a_parts, w, b.reshape(1, D), *x_parts, g.reshape(1, D), beta.reshape(1, D), rw, rb)


def _filter_kernel(z_ref, w_in_ref, b_in_ref, w_mid_ref, b_mid_ref, freq_ref, w_out_ref, delta_ref,
                   k_ref, nrm_ref, *, tl):
    r = pl.program_id(1)
    z = z_ref[...]
    h = jnp.sin(freq_ref[0:1, :] * (_dot3(z, w_in_ref[...]) + b_in_ref[...]))
    for j in range(FILTER_INNER):
        h = jnp.sin(freq_ref[j + 1:j + 2, :] * (_dot3(h, w_mid_ref[j]) + b_mid_ref[j:j + 1, :]))
    out = _dot3(h, w_out_ref[...]) * jnp.exp(-z[:, 0:1] * delta_ref[...])
    row = lax.broadcasted_iota(jnp.int32, out.shape, 0) + r * tl
    out = jnp.where(row == 0, 0.0, out)
    k_ref[0] = out

    @pl.when(r == 0)
    def _():
        nrm_ref[...] = jnp.zeros_like(nrm_ref)

    nrm_ref[0] = nrm_ref[0] + jnp.sum(jnp.abs(out), axis=0, keepdims=True)


def _hyena_filters(L, D, w_in, b_in, w_mid, b_mid, freq, w_out, tl=256):
    t = jnp.linspace(0.0, 1.0, L, dtype=F32)[:, None]
    omega = (2.0 * math.pi / L) * jnp.arange(L, dtype=F32)
    bands = jnp.linspace(1e-4, FILTER_BANDS - 1, FILTER_BANDS, dtype=F32)
    phase = omega[:, None] * bands[None, :]
    z = jnp.concatenate([t, jnp.cos(phase), -jnp.sin(phase)], axis=-1)
    z_lag = jnp.concatenate([z[:1], z[1:][::-1], z], axis=0)
    deltas = jnp.abs(jnp.linspace(MIN_DECAY, MAX_DECAY, D, dtype=F32)).reshape(1, D)
    tl = _tile(L, tl)
    n_neg = L // tl
    hid = w_in.shape[1]
    fixed2 = lambda n, r: (0, 0)
    return pl.pallas_call(
        functools.partial(_filter_kernel, tl=tl),
        grid=(HYENA_ORDER, 2 * L // tl),
        in_specs=[pl.BlockSpec((tl, FILTER_EMB), lambda n, r: (r, 0)),
                  pl.BlockSpec((FILTER_EMB, hid), fixed2), pl.BlockSpec((1, hid), fixed2),
                  pl.BlockSpec((FILTER_INNER, hid, hid), lambda n, r: (0, 0, 0)),
                  pl.BlockSpec((FILTER_INNER, hid), fixed2),
                  pl.BlockSpec((FILTER_INNER + 1, hid), fixed2),
                  pl.BlockSpec((hid, D), lambda n, r: (0, 2 * n + (r < n_neg).astype(jnp.int32))),
                  pl.BlockSpec((1, D), fixed2)],
        out_specs=[pl.BlockSpec((1, tl, D), lambda n, r: (n, r, 0)),
                   pl.BlockSpec((1, SUBLANES, D), lambda n, r: (n, 0, 0))],
        out_shape=[jax.ShapeDtypeStruct((HYENA_ORDER, 2 * L, D), F32),
                   jax.ShapeDtypeStruct((HYENA_ORDER, SUBLANES, D), F32)],
        compiler_params=_params(("arbitrary", "arbitrary")),
        name="hyena_filter",
    )(z_lag, w_in, b_in.reshape(1, hid), w_mid, b_mid, freq, w_out, deltas)


def _dft_matrices(bk):
    nfp = (bk + 1 + SUBLANES - 1) // SUBLANES * SUBLANES
    f = jnp.arange(nfp, dtype=jnp.int32)[:, None]
    r = jnp.arange(bk, dtype=jnp.int32)[None, :]
    ang = (2.0 * math.pi / (2 * bk)) * ((f * r) % (2 * bk)).astype(F32)
    live = f <= bk
    cos = jnp.where(live, jnp.cos(ang), 0.0)
    sin = jnp.where(live, jnp.sin(ang), 0.0)
    fwd = jnp.concatenate([cos, -sin], axis=0)
    wgt = jnp.where((f == 0) | (f == bk), 1.0, 2.0) / (2 * bk)
    inv = jnp.concatenate([cos * wgt, -sin * wgt], axis=0).T
    return fwd.astype(BF16), inv.astype(BF16), nfp


def _spectra_kernel(k_ref, nrm_ref, f_ref, c_ref, a_prev, k_prev, *, nfp):
    s = pl.program_id(2)
    kb = k_ref[0]
    a = jnp.dot(f_ref[...], kb.astype(BF16), preferred_element_type=F32)

    @pl.when(s > 0)
    def _():
        row = lax.broadcasted_iota(jnp.int32, a.shape, 0)
        sign = jnp.where((row & 1) == 0, 1.0, -1.0)
        k0 = jnp.where(row < nfp, k_prev[0:1, :], 0.0)
        c_ref[0, 0] = (a + sign * (a_prev[...] - k0)) / nrm_ref[0, 0:1, :]

    a_prev[...] = a
    k_prev[...] = kb[0:SUBLANES, :]


def _hyena_spectra(kappa, nrm, fwd, bk, nfp, dt=256):
    order, two_l, D = kappa.shape
    n_blk = two_l // bk
    dt = _tile(D, dt)
    return pl.pallas_call(
        functools.partial(_spectra_kernel, nfp=nfp),
        grid=(order, D // dt, n_blk),
        in_specs=[pl.BlockSpec((1, bk, dt), lambda n, d, s: (n, s, d)),
                  pl.BlockSpec((1, SUBLANES, dt), lambda n, d, s: (n, 0, d)),
                  pl.BlockSpec((2 * nfp, bk), lambda n, d, s: (0, 0))],
        out_specs=pl.BlockSpec((1, 1, 2 * nfp, dt), lambda n, d, s: (n, jnp.maximum(s - 1, 0), 0, d)),
        out_shape=jax.ShapeDtypeStruct((order, n_blk - 1, 2 * nfp, D), F32),
        scratch_shapes=[pltpu.VMEM((2 * nfp, dt), F32), pltpu.VMEM((SUBLANES, dt), F32)],
        compiler_params=_params(("arbitrary", "arbitrary", "arbitrary")),
        name="hyena_spectra",
    )(kappa, nrm, fwd)


def _short_conv_block(src_ref, w_ref, b_ref, j, nb, bk):
    cur = src_ref[0, j * bk:(j + 1) * bk, :]
    row = lax.broadcasted_iota(jnp.int32, cur.shape, 0)
    if j > 0:
        before = src_ref[0, j * bk - SUBLANES:j * bk, :][SUBLANES - 1:SUBLANES, :]
    else:
        before = jnp.zeros_like(cur[0:1, :])
    if j < nb - 1:
        after = src_ref[0, (j + 1) * bk:(j + 1) * bk + SUBLANES, :][0:1, :]
    else:
        after = jnp.zeros_like(cur[0:1, :])
    up = jnp.where(row == 0, before, pltpu.roll(cur, 1, 0))
    dn = jnp.where(row == bk - 1, after, pltpu.roll(cur, bk - 1, 0))
    return b_ref[...] + up * w_ref[0:1, :] + cur * w_ref[1:2, :] + dn * w_ref[2:3, :]


def _long_conv_kernel(*refs, nb, bk, nfp, conv_input):
    if conv_input:
        (z_ref, zw_ref, zb_ref, g_ref, gw_ref, gb_ref, skip_ref, c_ref, fwd_ref, inv_ref,
         o_ref, zf_ref, y_ref) = refs
    else:
        (z_ref, g_ref, gw_ref, gb_ref, skip_ref, c_ref, fwd_ref, inv_ref, o_ref, zf_ref, y_ref) = refs

    def z_block(j):
        if conv_input:
            return _short_conv_block(z_ref, zw_ref, zb_ref, j, nb, bk)
        return z_ref[0, j * bk:(j + 1) * bk, :]

    for j in range(nb):
        zf_ref[j] = jnp.dot(fwd_ref[...], z_block(j).astype(BF16), preferred_element_type=F32)

    for i in range(nb):
        def chunk(c, carry, i=i):
            r0 = pl.multiple_of(c * SUBLANES, SUBLANES)
            re_rows, im_rows = pl.ds(r0, SUBLANES), pl.ds(nfp + r0, SUBLANES)
            acc_re = acc_im = None
            for j in range(nb):
                m = i - j + nb - 1
                c_re, c_im = c_ref[0, m, re_rows, :], c_ref[0, m, im_rows, :]
                z_re, z_im = zf_ref[j, re_rows, :], zf_ref[j, im_rows, :]
                t_re = c_re * z_re - c_im * z_im
                t_im = c_re * z_im + c_im * z_re
                acc_re = t_re if acc_re is None else acc_re + t_re
                acc_im = t_im if acc_im is None else acc_im + t_im
            y_ref[re_rows, :] = acc_re
            y_ref[im_rows, :] = acc_im
            return carry

        lax.fori_loop(0, nfp // SUBLANES, chunk, 0, unroll=5)
        y = jnp.dot(inv_ref[...], y_ref[...].astype(BF16), preferred_element_type=F32)
        gate = _short_conv_block(g_ref, gw_ref, gb_ref, i, nb, bk)
        o_ref[0, i * bk:(i + 1) * bk, :] = (gate * (y + skip_ref[...] * z_block(i))).astype(o_ref.dtype)


def _long_conv(u3, z3, order, spectra, fwd, inv, conv_w, conv_b, skip, *, b0, nbatch, bk, nfp, dt, out_dtype):
    _, L, three_d = u3.shape
    D = three_d // 3
    nb = L // bk
    n_d = D // dt
    conv_input = z3 is None
    slab = lambda k: (lambda d, b: (b0 + b, 0, k * n_d + d))
    cols = lambda k: (lambda d, b: (0, k * n_d + d))
    seq_block = (1, L, dt)
    own = lambda d, b: (b, 0, d)
    in_specs, args = [], []
    if conv_input:
        in_specs += [pl.BlockSpec(seq_block, slab(0)), pl.BlockSpec((SHORT_CONV, dt), cols(0)),
                     pl.BlockSpec((1, dt), cols(0))]
        args += [u3, conv_w, conv_b]
    else:
        in_specs += [pl.BlockSpec(seq_block, own)]
        args += [z3]
    in_specs += [pl.BlockSpec(seq_block, slab(order + 1)), pl.BlockSpec((SHORT_CONV, dt), cols(order + 1)),
                 pl.BlockSpec((1, dt), cols(order + 1)),
                 pl.BlockSpec((1, dt), lambda d, b: (0, d)),
                 pl.BlockSpec((1, 2 * nb - 1, 2 * nfp, dt), lambda d, b: (order, 0, 0, d),
                              pipeline_mode=pl.Buffered(1)),
                 pl.BlockSpec((2 * nfp, bk), lambda d, b: (0, 0)),
                 pl.BlockSpec((bk, 2 * nfp), lambda d, b: (0, 0))]
    args += [u3, conv_w, conv_b, skip[order:order + 1], spectra, fwd, inv]
    return pl.pallas_call(
        functools.partial(_long_conv_kernel, nb=nb, bk=bk, nfp=nfp, conv_input=conv_input),
        grid=(n_d, nbatch),
        in_specs=in_specs,
        out_specs=pl.BlockSpec(seq_block, own),
        out_shape=jax.ShapeDtypeStruct((nbatch, L, D), out_dtype),
        scratch_shapes=[pltpu.VMEM((nb, 2 * nfp, dt), F32), pltpu.VMEM((2 * nfp, dt), F32)],
        compiler_params=_params(("arbitrary", "arbitrary")),
        name="hyena_long_conv",
    )(*args)


def _hyena_mixer(x_parts, groups, p):
    T, D = sum(x.shape[0] for x in x_parts), x_parts[0].shape[1]
    u = _project(x_parts, p["hy_w_in"].astype(BF16), p["hy_b_in"], F32)
    conv_w, conv_b, skip = p["hy_conv_w"], p["hy_conv_b"].reshape(1, 3 * D), p["hy_filt_skip"]
    outs = []
    for first, nbatch, L in groups:
        bk = min(L, HYENA_BLOCK)
        dt = _tile(D, 256 if L <= 2048 else 128)
        fwd, inv, nfp = _dft_matrices(bk)
        kappa, nrm = _hyena_filters(L, D, p["hy_filt_w_in"], p["hy_filt_b_in"], p["hy_filt_w_mid"],
                                    p["hy_filt_b_mid"], p["hy_filt_freq"], p["hy_filt_w_out"])
        spectra = _hyena_spectra(kappa, nrm, fwd, bk, nfp)
        kw = dict(b0=first // L, nbatch=nbatch, bk=bk, nfp=nfp, dt=dt)
        u3 = u.reshape(T // L, L, 3 * D)
        z1 = _long_conv(u3, None, 0, spectra, fwd, inv, conv_w, conv_b, skip, out_dtype=F32, **kw)
        z2 = _long_conv(u3, z1, 1, spectra, fwd, inv, conv_w, conv_b, skip, out_dtype=BF16, **kw)
        outs.append(z2.reshape(nbatch * L, D))
    return outs


def _attn_kernel(q_ref, k_ref, v_ref, lam_ref, g_ref, o_ref, *, lam_init, kc):
    dh = DIFF_HEAD_DIM
    lam_dots = jnp.sum(lam_ref[0:2, :] * lam_ref[2:4, :], axis=-1, keepdims=True)
    lam = jnp.exp(lam_dots[0:1, :]) - jnp.exp(lam_dots[1:2, :]) + lam_init
    q = q_ref[...]
    n_chunks = k_ref.shape[0] // kc
    state = [None, None]
    for j in range(n_chunks):
        rows = slice(j * kc, (j + 1) * kc)
        v = v_ref[rows, :]
        for c in range(2):
            s = lax.dot_general(q[:, c * dh:(c + 1) * dh], k_ref[rows, c * dh:(c + 1) * dh],
                                (((1,), (1,)), ((), ())), preferred_element_type=F32)
            top = jnp.max(s, axis=-1, keepdims=True)
            if state[c] is None:
                e = jnp.exp2(s - top)
                state[c] = (top, jnp.sum(e, axis=-1, keepdims=True),
                            jnp.dot(e.astype(BF16), v, preferred_element_type=F32))
            else:
                m_old, l_old, acc = state[c]
                m_new = jnp.maximum(m_old, top)
                shrink = jnp.exp2(m_old - m_new)
                e = jnp.exp2(s - m_new)
                state[c] = (m_new, shrink * l_old + jnp.sum(e, axis=-1, keepdims=True),
                            shrink * acc + jnp.dot(e.astype(BF16), v, preferred_element_type=F32))
    (_, l0, o0), (_, l1, o1) = state
    o = o0 * (1.0 / l0) - o1 * (lam / l1)
    o = o * lax.rsqrt(jnp.mean(o * o, axis=-1, keepdims=True) + LN_EPS) * g_ref[...]
    o_ref[...] = (o * (1.0 - lam_init)).astype(o_ref.dtype)


def _diff_attention(qkv, lam_vecs, subln_g, layer_idx, *, first, nbatch, L, tq, kc):
    T, three_d = qkv.shape
    D = three_d // 3
    hw = 2 * DIFF_HEAD_DIM
    H = D // hw
    tq, kc = _tile(L, tq), _tile(L, kc)
    nq = L // tq
    q0, b0 = first // tq, first // L
    lam_init = 0.8 - 0.6 * math.exp(-0.3 * layer_idx)
    return pl.pallas_call(
        functools.partial(_attn_kernel, lam_init=lam_init, kc=kc),
        grid=(nbatch, H, nq),
        in_specs=[pl.BlockSpec((tq, hw), lambda b, h, i: (q0 + b * nq + i, h)),
                  pl.BlockSpec((L, hw), lambda b, h, i: (b0 + b, H + h)),
                  pl.BlockSpec((L, hw), lambda b, h, i: (b0 + b, 2 * H + h)),
                  pl.BlockSpec((4, DIFF_HEAD_DIM), lambda b, h, i: (0, 0)),
                  pl.BlockSpec((1, hw), lambda b, h, i: (0, 0))],
        out_specs=pl.BlockSpec((tq, hw), lambda b, h, i: (b * nq + i, h)),
        out_shape=jax.ShapeDtypeStruct((nbatch * L, D), BF16),
        compiler_params=_params(("arbitrary", "arbitrary", "arbitrary")),
        name="diff_attention",
    )(qkv, qkv, qkv, lam_vecs, subln_g.reshape(1, hw))


def _attention_mixer(x, groups, p, layer_idx):
    split = groups[1][0] if len(groups) > 1 else x.shape[0]
    l1, l2 = groups[0][2], groups[-1][2]
    qkv = _qkv_project(x, p["da_w_qkv"].astype(BF16), split, l1, l2)
    lam_vecs = jnp.stack([p["da_lambda_q1"], p["da_lambda_q2"], p["da_lambda_k1"], p["da_lambda_k2"]])
    return [_diff_attention(qkv, lam_vecs, p["da_subln_g"], layer_idx, first=first, nbatch=nbatch, L=L,
                            tq=ATTN_Q_TILE * (1 if L > 2048 else 2), kc=ATTN_KEY_CHUNK // (1 if L > 2048 else 2))
            for first, nbatch, L in groups]


def _row_copy(src, src_row, dst, dst_row, sem):
    return pltpu.make_async_copy(src.at[pl.ds(src_row, 1)], dst.at[pl.ds(dst_row, 1)], sem)


def _dispatch_kernel(pend_ref, padded_ref, dest_ref, x_ref, xs_out, zeros, sem, zero_sem, *, tm, rb, n_experts):
    @pl.when(pl.program_id(0) == 0)
    def _():
        zeros[...] = jnp.zeros_like(zeros)

        def last_tile(e):
            first_row = pl.multiple_of(pend_ref[e] - rb, rb)
            return pltpu.make_async_copy(zeros, xs_out.at[pl.ds(first_row, rb)], zero_sem)

        for e in range(n_experts):
            @pl.when(padded_ref[e] > 0)
            def _(e=e):
                last_tile(e).start()

        for e in range(n_experts):
            @pl.when(padded_ref[e] > 0)
            def _(e=e):
                last_tile(e).wait()

    def issue(group, carry):
        base = pl.multiple_of(group * SUBLANES, SUBLANES)
        for u in range(SUBLANES):
            for k in range(TOP_K):
                _row_copy(x_ref, base + u, xs_out, dest_ref[(base + u) * TOP_K + k], sem).start(priority=k % 2)
        return carry

    lax.fori_loop(0, tm // SUBLANES, issue, 0)
    done = xs_out.at[pl.ds(0, TOP_K * tm)]
    pltpu.make_async_copy(done, done, sem).wait()


def _dispatch(x_packed, dest_flat, padded_end, padded, n_rows, rb, tm=256):
    T, half = x_packed.shape
    tm = _tile(T, tm)
    return pl.pallas_call(
        functools.partial(_dispatch_kernel, tm=tm, rb=rb, n_experts=padded.shape[0]),
        grid_spec=pltpu.PrefetchScalarGridSpec(
            num_scalar_prefetch=2,
            grid=(T // tm,),
            in_specs=[pl.BlockSpec((tm * TOP_K,), lambda i, pe, pd: (i,), memory_space=pltpu.SMEM),
                      pl.BlockSpec((tm, half), lambda i, pe, pd: (i, 0))],
            out_specs=pl.BlockSpec(memory_space=pl.ANY),
            scratch_shapes=[pltpu.VMEM((rb, half), jnp.uint32), pltpu.SemaphoreType.DMA(()),
                            pltpu.SemaphoreType.DMA(())]),
        out_shape=jax.ShapeDtypeStruct((n_rows, half), jnp.uint32),
        compiler_params=_params(("arbitrary",)),
        name="moe_dispatch",
    )(padded_end, padded, dest_flat, x_packed)


def _expert_kernel(tile_expert_ref, n_used_ref, x_ref, bg_ref, bu_ref, bd_ref, wg_hbm, wu_hbm, wd_hbm, o_ref,
                   wg, wu, wd, stage, sem, *, chunk):
    i = pl.program_id(0)
    expert = tile_expert_ref[i]

    @pl.when(i < n_used_ref[0])
    def _():
        @pl.when((i == 0) | (tile_expert_ref[jnp.maximum(i - 1, 0)] != expert))
        def _():
            pieces = [(src, dst, r) for src, dst in ((wg_hbm, wg), (wu_hbm, wu), (wd_hbm, wd))
                      for r in range(0, dst.shape[0], chunk)]

            def fetch(n):
                src, _, r = pieces[n]
                return pltpu.make_async_copy(src.at[expert, pl.ds(r, chunk)], stage.at[n % 2], sem.at[n % 2])

            fetch(0).start()
            for n, (_, dst, r) in enumerate(pieces):
                if n + 1 < len(pieces):
                    fetch(n + 1).start()
                fetch(n).wait()
                dst[r:r + chunk, :] = stage[n % 2].astype(BF16)

        x = _unpack_bf16_pairs(x_ref[...])
        gate = jnp.dot(x, wg[...], preferred_element_type=F32) + bg_ref[0]
        up = jnp.dot(x, wu[...], preferred_element_type=F32) + bu_ref[0]
        gate = jnp.minimum(gate, SWIGLU_LIMIT)
        up = jnp.clip(up, -SWIGLU_LIMIT, SWIGLU_LIMIT)
        h = gate * (1.0 / (1.0 + jnp.exp(-SWIGLU_ALPHA * gate))) * (up + 1.0)
        y = jnp.dot(h.astype(BF16), wd[...], preferred_element_type=F32) + bd_ref[0]
        o_ref[...] = _pack_bf16_pairs(y)


def _experts(xs, tile_expert, n_used, wg, bg, wu, bu, wd, bd, rb, chunk=512):
    P, half = xs.shape
    E, D, F = wg.shape
    assert D == F, "one staging buffer shape serves all three weight matrices"
    n_tiles = P // rb
    chunk = _tile(D, chunk)
    of_expert = lambda i, te, nu: (te[i], 0, 0)
    hbm = pl.BlockSpec(memory_space=pl.ANY)
    return pl.pallas_call(
        functools.partial(_expert_kernel, chunk=chunk),
        grid_spec=pltpu.PrefetchScalarGridSpec(
            num_scalar_prefetch=2,
            grid=(n_tiles,),
            in_specs=[pl.BlockSpec((rb, half), lambda i, te, nu: (jnp.minimum(i, nu[0] - 1), 0)),
                      pl.BlockSpec((1, 1, F), of_expert), pl.BlockSpec((1, 1, F), of_expert),
                      pl.BlockSpec((1, 1, D), of_expert), hbm, hbm, hbm],
            out_specs=pl.BlockSpec((rb, half), lambda i, te, nu: (i, 0)),
            scratch_shapes=[pltpu.VMEM((D, F), BF16), pltpu.VMEM((D, F), BF16), pltpu.VMEM((F, D), BF16),
                            pltpu.VMEM((2, chunk, F), F32), pltpu.SemaphoreType.DMA((2,))]),
        out_shape=jax.ShapeDtypeStruct((P, half), jnp.uint32),
        compiler_params=_params(("arbitrary",)),
        name="moe_experts",
    )(tile_expert, n_used, xs, bg, bu, bd, wg, wu, wd)


def _combine_kernel(dest_ref, dest_next_ref, gate_ref, x_ref, g_ref, beta_ref, ys_hbm, *rest,
                    tc, n_steps, n_first):
    *o_refs, rows, sem = rest
    i = pl.program_id(0)
    slot = lax.rem(i, 2)

    def gather(d_ref, s):
        def issue(group, carry):
            base = pl.multiple_of(group * SUBLANES, SUBLANES)
            for u in range(SUBLANES):
                for k in range(TOP_K):
                    _row_copy(ys_hbm, d_ref[(base + u) * TOP_K + k], rows.at[s, k], base + u,
                              sem.at[s]).start(priority=k % 2)
            return carry
        lax.fori_loop(0, tc // SUBLANES, issue, 0)

    @pl.when(i == 0)
    def _():
        gather(dest_ref, 0)

    @pl.when(i + 1 < n_steps)
    def _():
        gather(dest_next_ref, 1 - slot)

    pltpu.make_async_copy(rows.at[slot], rows.at[slot], sem.at[slot]).wait()
    gates = gate_ref[...]
    mix_hi = mix_lo = None
    for k in range(TOP_K):
        w = rows[slot, k]
        hi = gates[:, k:k + 1] * pltpu.bitcast(w & jnp.uint32(0xFFFF0000), F32)
        lo = gates[:, k:k + 1] * pltpu.bitcast(w << 16, F32)
        mix_hi = hi if mix_hi is None else mix_hi + hi
        mix_lo = lo if mix_lo is None else mix_lo + lo
    mix = jnp.concatenate([mix_hi, mix_lo], axis=1)
    y = _layer_norm(ALPHA * x_ref[...] + mix, g_ref[...], beta_ref[...])
    if n_first is None:
        o_refs[0][...] = y
    else:
        @pl.when(i < n_first)
        def _():
            o_refs[0][...] = y

        @pl.when(i >= n_first)
        def _():
            o_refs[1][...] = y


def _combine_ln(ys, dest_flat, gates, x, g, beta, split=None, tc=128):
    T, D = x.shape
    tc = _tile(T if split is None else math.gcd(split, T - split), tc)
    n_steps = T // tc
    row = lambda i: (i, 0)
    fixed = lambda i: (0, 0)
    tile = (tc, D)
    if split is None:
        n_first = None
        out_specs = pl.BlockSpec(tile, row)
        out_shape = jax.ShapeDtypeStruct((T, D), F32)
    else:
        n_first = split // tc
        out_specs = [pl.BlockSpec(tile, lambda i: (jnp.minimum(i, n_first - 1), 0)),
                     pl.BlockSpec(tile, lambda i: (jnp.maximum(i - n_first, 0), 0))]
        out_shape = [jax.ShapeDtypeStruct((split, D), F32), jax.ShapeDtypeStruct((T - split, D), F32)]
    return pl.pallas_call(
        functools.partial(_combine_kernel, tc=tc, n_steps=n_steps, n_first=n_first),
        grid=(n_steps,),
        in_specs=[pl.BlockSpec((tc * TOP_K,), lambda i: (i,), memory_space=pltpu.SMEM),
                  pl.BlockSpec((tc * TOP_K,), lambda i: (jnp.minimum(i + 1, n_steps - 1),),
                               memory_space=pltpu.SMEM),
                  pl.BlockSpec((tc, LANES), row), pl.BlockSpec(tile, row),
                  pl.BlockSpec((1, D), fixed), pl.BlockSpec((1, D), fixed),
                  pl.BlockSpec(memory_space=pl.ANY)],
        out_specs=out_specs,
        out_shape=out_shape,
        scratch_shapes=[pltpu.VMEM((2, TOP_K, tc, D // 2), jnp.uint32), pltpu.SemaphoreType.DMA((2,))],
        compiler_params=_params(("arbitrary",)),
        name="moe_combine_ln",
    )(dest_flat, dest_flat, gates, x, g.reshape(1, D), beta.reshape(1, D), ys)


def _expert_stacks(p):
    n_layers, E, D, F = p["moe_w_gate"].shape
    stack = lambda w: w.reshape((n_layers * E,) + w.shape[2:])
    bias = lambda b: b.reshape(n_layers * E, 1, b.shape[-1])
    return (stack(p["moe_w_gate"]), bias(p["moe_b_gate"]), stack(p["moe_w_up"]), bias(p["moe_b_up"]),
            stack(p["moe_w_down"]), bias(p["moe_b_down"]))


def _moe_ln(x, x_packed, routing, p, stacks, i, rb, split=None):
    T, D = x.shape
    E = p["moe_router_w"].shape[-1]
    idx, gates, rank, counts = routing
    counts = counts[0, :E].astype(jnp.int32)
    padded = (counts + rb - 1) // rb * rb
    padded_end = jnp.cumsum(padded)
    padded_start = padded_end - padded
    n_tiles = (T * TOP_K) // rb + E
    tile_expert = jnp.minimum(jnp.searchsorted(padded_end, jnp.arange(n_tiles, dtype=jnp.int32) * rb, side="right"),
                              E - 1).astype(jnp.int32)
    n_used = (padded_end[-1:] // rb).astype(jnp.int32)
    dest = (padded_start[idx[:, :TOP_K]] + rank[:, :TOP_K]).reshape(T * TOP_K)
    xs = _dispatch(x_packed, dest, padded_end.astype(jnp.int32), padded, n_tiles * rb, rb)
    ys = _experts(xs, tile_expert + i * E, n_used, *stacks, rb)
    return _combine_ln(ys, dest, gates, x, p["ln2_g"][i], p["ln2_b"][i], split)


def _encoder(x_parts, groups, p, rb):
    stacks = _expert_stacks(p)
    D = x_parts[0].shape[1]
    for i in range(DEPTH):
        j = i // 2
        if i % 2 == 0:
            pj = {k: v[j] for k, v in p.items() if k.startswith("hy_")}
            a_parts = _hyena_mixer(x_parts, groups, pj)
            w_o, b_o = pj["hy_w_out"], pj["hy_b_out"]
        else:
            pj = {k: v[j] for k, v in p.items() if k.startswith("da_")}
            a_parts = _attention_mixer(x_parts[0], groups, pj, i)
            w_o, b_o = pj["da_w_o"], jnp.zeros((D,), F32)
        x, x_packed, *routing = _out_project_ln_route(a_parts, w_o.astype(BF16), b_o, x_parts, p["ln1_g"][i],
                                                      p["ln1_b"][i], p["moe_router_w"][i], p["moe_router_b"][i])
        last = i == DEPTH - 1
        y = _moe_ln(x, x_packed, routing, p, stacks, i, rb, groups[1][0] if last and len(groups) > 1 else None)
        x_parts = list(y) if isinstance(y, (list, tuple)) else [y]
    return x_parts


def _forward(x_prompt, x_sample, p, rb=512):
    B1, L1, D = x_prompt.shape
    B2, L2, _ = x_sample.shape
    T1, T2 = B1 * L1, B2 * L2
    assert T1 % L2 == 0, "sample sequences must start on a multiple of their length"
    y1, y2 = _encoder([x_prompt.reshape(T1, D), x_sample.reshape(T2, D)], [(0, B1, L1), (T1, B2, L2)], p, rb)
    return y1.reshape(B1, L1, D), y2.reshape(B2, L2, D)


def kernel(x_prompt, x_sample, hy_w_in, hy_b_in, hy_conv_w, hy_conv_b, hy_filt_w_in, hy_filt_b_in, hy_filt_w_mid, hy_filt_b_mid, hy_filt_freq, hy_filt_w_out, hy_filt_skip, hy_w_out, hy_b_out, da_w_qkv, da_lambda_q1, da_lambda_k1, da_lambda_q2, da_lambda_k2, da_subln_g, da_w_o, ln1_g, ln1_b, moe_router_w, moe_router_b, moe_w_gate, moe_b_gate, moe_w_up, moe_b_up, moe_w_down, moe_b_down, ln2_g, ln2_b):
    p = dict(hy_w_in=hy_w_in, hy_b_in=hy_b_in, hy_conv_w=hy_conv_w, hy_conv_b=hy_conv_b,
             hy_filt_w_in=hy_filt_w_in, hy_filt_b_in=hy_filt_b_in, hy_filt_w_mid=hy_filt_w_mid,
             hy_filt_b_mid=hy_filt_b_mid, hy_filt_freq=hy_filt_freq, hy_filt_w_out=hy_filt_w_out,
             hy_filt_skip=hy_filt_skip, hy_w_out=hy_w_out, hy_b_out=hy_b_out,
             da_w_qkv=da_w_qkv, da_lambda_q1=da_lambda_q1, da_lambda_k1=da_lambda_k1,
             da_lambda_q2=da_lambda_q2, da_lambda_k2=da_lambda_k2, da_subln_g=da_subln_g, da_w_o=da_w_o,
             ln1_g=ln1_g, ln1_b=ln1_b, moe_router_w=moe_router_w, moe_router_b=moe_router_b,
             moe_w_gate=moe_w_gate, moe_b_gate=moe_b_gate, moe_w_up=moe_w_up, moe_b_up=moe_b_up,
             moe_w_down=moe_w_down, moe_b_down=moe_b_down, ln2_g=ln2_g, ln2_b=ln2_b)
    return _forward(x_prompt, x_sample, p)
```

```python
import functools
import math

import jax
import jax.numpy as jnp
from jax import lax
from jax.experimental import pallas as pl
from jax.experimental.pallas import tpu as pltpu

DEPTH = 2
SHORT_CONV = 3
HYENA_ORDER = 2
FILTER_EMB = 33
FILTER_BANDS = (FILTER_EMB - 1) // 2
FILTER_INNER = 2
FAST_DECAY_PCT = 0.3
SLOW_DECAY_PCT = 1.5
DECAY_TARGET = 1e-2
MAX_DECAY = math.log(DECAY_TARGET) / FAST_DECAY_PCT
MIN_DECAY = math.log(DECAY_TARGET) / SLOW_DECAY_PCT
DIFF_HEAD_DIM = 128
ROPE_DIM = DIFF_HEAD_DIM // 4
ROPE_THETA = 500000.0
TOP_K = 4
SWIGLU_ALPHA = 1.702
SWIGLU_LIMIT = 7.0
LN_EPS = 1e-5
ALPHA = (2 * DEPTH) ** 0.25

LANES = 128
SUBLANES = 8
VMEM_LIMIT_BYTES = 60 * 1024 * 1024

HYENA_BLOCK = 512
HYENA_CHANNEL_TILE = 256
HYENA_LONG_SEQ = 2048

ATTN_Q_TILE = 512
ATTN_KEY_CHUNK = 2048

F32 = jnp.float32
BF16 = jnp.bfloat16


def _params(semantics, vmem=VMEM_LIMIT_BYTES):
    return pltpu.CompilerParams(dimension_semantics=semantics, vmem_limit_bytes=vmem)


def _tile(n, pref):
    t = min(n, pref)
    assert n % t == 0, (n, t)
    return t


def _split_bf16(a):
    hi = a.astype(BF16)
    lo = (a - hi.astype(F32)).astype(BF16)
    return hi, lo


def _dot3(a, b):
    ah, al = _split_bf16(a)
    bh, bl = _split_bf16(b)
    d = functools.partial(jnp.dot, preferred_element_type=F32)
    return d(ah, bh) + d(al, bh) + d(ah, bl)


def _layer_norm(y, g, b):
    mu = jnp.mean(y, axis=-1, keepdims=True)
    yc = y - mu
    var = jnp.mean(yc * yc, axis=-1, keepdims=True)
    return yc * lax.rsqrt(var + LN_EPS) * g + b


def _row_sources(arrays, tm):
    specs, starts, lo = [], [], 0
    for arr in arrays:
        n = arr.shape[0] // tm
        assert n * tm == arr.shape[0], (arr.shape, tm)
        specs.append(pl.BlockSpec((tm, arr.shape[1]), lambda *idx, lo=lo, n=n: (jnp.clip(idx[-1] - lo, 0, n - 1), 0)))
        starts.append(lo)
        lo += n
    return specs, tuple(starts)


def _select_rows(refs, starts, i):
    rows = refs[0][...]
    for ref, lo in zip(refs[1:], starts[1:]):
        rows = jnp.where(i >= lo, ref[...], rows)
    return rows


def _proj_kernel(*refs, starts):
    *x_refs, w_ref, b_ref, o_ref = refs
    x = _select_rows(x_refs, starts, pl.program_id(1))
    acc = jnp.dot(x.astype(BF16), w_ref[...], preferred_element_type=F32)
    o_ref[...] = (acc + b_ref[...]).astype(o_ref.dtype)


def _project(xs, w, b, out_dtype, tm=512, tn=2048):
    T = sum(x.shape[0] for x in xs)
    K, N = w.shape
    tm, tn = _tile(math.gcd(*[x.shape[0] for x in xs]), tm), _tile(N, tn)
    x_specs, starts = _row_sources(xs, tm)
    return pl.pallas_call(
        functools.partial(_proj_kernel, starts=starts),
        grid=(N // tn, T // tm),
        in_specs=x_specs + [pl.BlockSpec((K, tn), lambda n, m: (0, n)),
                            pl.BlockSpec((1, tn), lambda n, m: (0, n))],
        out_specs=pl.BlockSpec((tm, tn), lambda n, m: (m, n)),
        out_shape=jax.ShapeDtypeStruct((T, N), out_dtype),
        compiler_params=_params(("arbitrary", "arbitrary")),
        name="project",
    )(*xs, w, b.reshape(1, N))


def _qkv_kernel(x_ref, w_ref, rope_ref, o_ref, *, n_rope_slabs):
    n = pl.program_id(0)
    acc = jnp.dot(x_ref[...].astype(BF16), w_ref[...], preferred_element_type=F32)
    tn = acc.shape[1]

    @pl.when(n >= n_rope_slabs)
    def _():
        o_ref[...] = acc.astype(o_ref.dtype)

    @pl.when(n < n_rope_slabs)
    def _():
        scale = jnp.where(n == 0, DIFF_HEAD_DIM ** -0.5 * math.log2(math.e), 1.0)
        c_tab, s_lo, s_hi = rope_ref[0] * scale, rope_ref[1] * scale, rope_ref[2] * scale
        half = ROPE_DIM // 2
        for h in range(tn // LANES):
            xh = acc[:, h * LANES:(h + 1) * LANES]
            rot = (xh * c_tab + pltpu.roll(xh, LANES - half, 1) * s_lo
                   + pltpu.roll(xh, half, 1) * s_hi)
            o_ref[:, h * LANES:(h + 1) * LANES] = rot.astype(o_ref.dtype)


def _rope_tables(n_pos):
    half = ROPE_DIM // 2
    inv_freq = ROPE_THETA ** (-jnp.arange(0, ROPE_DIM, 2, dtype=F32) / ROPE_DIM)
    ang = jnp.arange(n_pos, dtype=F32)[:, None] * inv_freq[None, :]
    cos, sin = jnp.cos(ang), jnp.sin(ang)
    rest = LANES - ROPE_DIM
    pad = lambda parts, fill: jnp.concatenate(parts + [jnp.full((n_pos, rest), fill, F32)], axis=1)
    zero = jnp.zeros_like(sin)
    return jnp.stack([pad([cos, cos], 1.0), pad([-sin, zero], 0.0), pad([zero, sin], 0.0)])


def _qkv_project(x, w, split, l1, l2, tm=512):
    T, K = x.shape
    N = w.shape[1]
    tn = N // 3
    tm = _tile(math.gcd(l1, l2), tm)

    def pos_block(m):
        start = m * tm
        pos0 = jnp.where(start < split, lax.rem(start, l1), lax.rem(jnp.maximum(start - split, 0), l2))
        return pos0 // tm

    return pl.pallas_call(
        functools.partial(_qkv_kernel, n_rope_slabs=2),
        grid=(3, T // tm),
        in_specs=[pl.BlockSpec((tm, K), lambda n, m: (m, 0)),
                  pl.BlockSpec((K, tn), lambda n, m: (0, n)),
                  pl.BlockSpec((3, tm, LANES), lambda n, m: (0, pos_block(m), 0))],
        out_specs=pl.BlockSpec((tm, tn), lambda n, m: (m, n)),
        out_shape=jax.ShapeDtypeStruct((T, N), BF16),
        compiler_params=_params(("arbitrary", "arbitrary")),
        name="qkv_project",
    )(x, w, _rope_tables(max(l1, l2)))


def _pack_bf16_pairs(y):
    half = y.shape[1] // 2
    r = pltpu.bitcast(y.astype(BF16).astype(F32), jnp.uint32)
    return (r[:, :half] & jnp.uint32(0xFFFF0000)) | (r[:, half:] >> 16)


def _unpack_bf16_pairs(w):
    hi = pltpu.bitcast(w & jnp.uint32(0xFFFF0000), F32)
    lo = pltpu.bitcast(w << 16, F32)
    return jnp.concatenate([hi, lo], axis=1).astype(BF16)


def _route_rows(x, w_ref, b_ref, idx_ref, gate_ref, rank_ref, cnt_ref, base_ref, n_experts):
    @pl.when(pl.program_id(0) == 0)
    def _():
        base_ref[...] = jnp.zeros_like(base_ref)

    logits = _dot3(x, w_ref[...]) + b_ref[...]
    tm = logits.shape[0]
    lane = lax.broadcasted_iota(jnp.int32, (tm, LANES), 1)
    neg_inf = jnp.float32(-jnp.inf)
    lane_f = lane.astype(F32)
    live = jnp.where(lane < n_experts, logits, neg_inf)
    vals, picks, hots = [], [], []
    for _ in range(TOP_K):
        best = jnp.max(live, axis=-1, keepdims=True)
        pick = jnp.min(jnp.where(live == best, lane_f, float(LANES)), axis=-1, keepdims=True)
        hot = lane_f == pick
        live = jnp.where(hot, neg_inf, live)
        vals.append(best)
        picks.append(pick.astype(jnp.int32))
        hots.append(hot)
    exps = [jnp.exp(v - vals[0]) for v in vals]
    denom = exps[0] + exps[1] + exps[2] + exps[3]
    chosen = jnp.zeros((tm, LANES), F32)
    for hot in hots:
        chosen = chosen + hot.astype(F32)
    r_i = lax.broadcasted_iota(jnp.int32, (tm, tm), 0)
    c_i = lax.broadcasted_iota(jnp.int32, (tm, tm), 1)
    earlier = jnp.where(r_i > c_i, 1.0, 0.0).astype(BF16)
    before = base_ref[0:1, :] + jnp.dot(earlier, chosen.astype(BF16), preferred_element_type=F32)
    idx_out = jnp.zeros((tm, LANES), jnp.int32)
    rank_out = jnp.zeros((tm, LANES), jnp.int32)
    gate_out = jnp.zeros((tm, LANES), F32)
    for k in range(TOP_K):
        rank_k = jnp.sum(jnp.where(hots[k], before, 0.0), axis=-1, keepdims=True).astype(jnp.int32)
        idx_out = jnp.where(lane == k, picks[k], idx_out)
        rank_out = jnp.where(lane == k, rank_k, rank_out)
        gate_out = jnp.where(lane == k, exps[k] / denom, gate_out)
    idx_ref[...] = idx_out
    rank_ref[...] = rank_out
    gate_ref[...] = gate_out
    base_ref[...] = base_ref[...] + jnp.sum(chosen, axis=0, keepdims=True)
    cnt_ref[...] = base_ref[...]


def _out_ln_route_kernel(*refs, n_a, n_x, a_starts, x_starts, n_experts):
    a_refs, refs = refs[:n_a], refs[n_a:]
    (w_ref, b_ref), refs = refs[:2], refs[2:]
    x_refs, refs = refs[:n_x], refs[n_x:]
    g_ref, beta_ref, rw_ref, rb_ref, o_ref, p_ref, idx_ref, gate_ref, rank_ref, cnt_ref, base_ref = refs
    i = pl.program_id(0)
    h = jnp.dot(_select_rows(a_refs, a_starts, i), w_ref[...], preferred_element_type=F32) + b_ref[...]
    y = _layer_norm(ALPHA * _select_rows(x_refs, x_starts, i) + h, g_ref[...], beta_ref[...])
    o_ref[...] = y
    p_ref[...] = _pack_bf16_pairs(y)
    _route_rows(y, rw_ref, rb_ref, idx_ref, gate_ref, rank_ref, cnt_ref, base_ref, n_experts)


def _out_project_ln_route(a_parts, w, b, x_parts, g, beta, router_w, router_b, tm=512):
    T = sum(a.shape[0] for a in a_parts)
    K, D = w.shape
    E = router_w.shape[1]
    tm = _tile(math.gcd(*[v.shape[0] for v in a_parts + x_parts]), tm)
    a_specs, a_starts = _row_sources(a_parts, tm)
    x_specs, x_starts = _row_sources(x_parts, tm)
    row = lambda i: (i, 0)
    fixed = lambda i: (0, 0)
    once = dict(pipeline_mode=pl.Buffered(1))
    wide = pl.BlockSpec((tm, LANES), row)
    rw = jnp.pad(router_w, ((0, 0), (0, LANES - E)))
    rb = jnp.pad(router_b, (0, LANES - E)).reshape(1, LANES)
    return pl.pallas_call(
        functools.partial(_out_ln_route_kernel, n_a=len(a_parts), n_x=len(x_parts), a_starts=a_starts,
                          x_starts=x_starts, n_experts=E),
        grid=(T // tm,),
        in_specs=(a_specs + [pl.BlockSpec((K, D), fixed, **once), pl.BlockSpec((1, D), fixed)] + x_specs
                  + [pl.BlockSpec((1, D), fixed), pl.BlockSpec((1, D), fixed),
                     pl.BlockSpec((D, LANES), fixed, **once), pl.BlockSpec((1, LANES), fixed)]),
        out_specs=[pl.BlockSpec((tm, D), row), pl.BlockSpec((tm, D // 2), row), wide, wide, wide,
                   pl.BlockSpec((SUBLANES, LANES), fixed)],
        out_shape=[jax.ShapeDtypeStruct((T, D), F32), jax.ShapeDtypeStruct((T, D // 2), jnp.uint32),
                   jax.ShapeDtypeStruct((T, LANES), jnp.int32), jax.ShapeDtypeStruct((T, LANES), F32),
                   jax.ShapeDtypeStruct((T, LANES), jnp.int32), jax.ShapeDtypeStruct((SUBLANES, LANES), F32)],
        scratch_shapes=[pltpu.VMEM((SUBLANES, LANES), F32)],
        compiler_params=_params(("arbitrary",)),
        name="out_project_ln_route",
    )(*a_parts, w, b.reshape(1, D), *x_parts, g.reshape(1, D), beta.reshape(1, D), rw, rb)


def _filter_kernel(z_ref, w_in_ref, b_in_ref, w_mid_ref, b_mid_ref, freq_ref, w_out_ref, delta_ref,
                   k_ref, nrm_ref, *, tl):
    r = pl.program_id(1)
    z = z_ref[...]
    h = jnp.sin(freq_ref[0:1, :] * (_dot3(z, w_in_ref[...]) + b_in_ref[...]))
    for j in range(FILTER_INNER):
        h = jnp.sin(freq_ref[j + 1:j + 2, :] * (_dot3(h, w_mid_ref[j]) + b_mid_ref[j:j + 1, :]))
    out = _dot3(h, w_out_ref[...]) * jnp.exp(-z[:, 0:1] * delta_ref[...])
    row = lax.broadcasted_iota(jnp.int32, out.shape, 0) + r * tl
    out = jnp.where(row == 0, 0.0, out)
    k_ref[0] = out

    @pl.when(r == 0)
    def _():
        nrm_ref[...] = jnp.zeros_like(nrm_ref)

    nrm_ref[0] = nrm_ref[0] + jnp.sum(jnp.abs(out), axis=0, keepdims=True)


def _hyena_filters(L, D, w_in, b_in, w_mid, b_mid, freq, w_out, tl=1024):
    t = jnp.linspace(0.0, 1.0, L, dtype=F32)[:, None]
    omega = (2.0 * math.pi / L) * jnp.arange(L, dtype=F32)
    bands = jnp.linspace(1e-4, FILTER_BANDS - 1, FILTER_BANDS, dtype=F32)
    phase = omega[:, None] * bands[None, :]
    z = jnp.concatenate([t, jnp.cos(phase), -jnp.sin(phase)], axis=-1)
    z_lag = jnp.concatenate([z[:1], z[1:][::-1], z], axis=0)
    deltas = jnp.abs(jnp.linspace(MIN_DECAY, MAX_DECAY, D, dtype=F32)).reshape(1, D)
    tl = _tile(L, tl)
    n_neg = L // tl
    hid = w_in.shape[1]
    fixed2 = lambda n, r: (0, 0)
    return pl.pallas_call(
        functools.partial(_filter_kernel, tl=tl),
        grid=(HYENA_ORDER, 2 * L // tl),
        in_specs=[pl.BlockSpec((tl, FILTER_EMB), lambda n, r: (r, 0)),
                  pl.BlockSpec((FILTER_EMB, hid), fixed2), pl.BlockSpec((1, hid), fixed2),
                  pl.BlockSpec((FILTER_INNER, hid, hid), lambda n, r: (0, 0, 0)),
                  pl.BlockSpec((FILTER_INNER, hid), fixed2),
                  pl.BlockSpec((FILTER_INNER + 1, hid), fixed2),
                  pl.BlockSpec((hid, D), lambda n, r: (0, 2 * n + (r < n_neg).astype(jnp.int32))),
                  pl.BlockSpec((1, D), fixed2)],
        out_specs=[pl.BlockSpec((1, tl, D), lambda n, r: (n, r, 0)),
                   pl.BlockSpec((1, SUBLANES, D), lambda n, r: (n, 0, 0))],
        out_shape=[jax.ShapeDtypeStruct((HYENA_ORDER, 2 * L, D), F32),
                   jax.ShapeDtypeStruct((HYENA_ORDER, SUBLANES, D), F32)],
        compiler_params=_params(("arbitrary", "arbitrary")),
        name="hyena_filter",
    )(z_lag, w_in, b_in.reshape(1, hid), w_mid, b_mid, freq, w_out, deltas)


def _dft_matrices(bk):
    nfp = (bk + 1 + SUBLANES - 1) // SUBLANES * SUBLANES
    f = jnp.arange(nfp, dtype=jnp.int32)[:, None]
    r = jnp.arange(bk, dtype=jnp.int32)[None, :]
    ang = (2.0 * math.pi / (2 * bk)) * ((f * r) % (2 * bk)).astype(F32)
    live = f <= bk
    cos = jnp.where(live, jnp.cos(ang), 0.0)
    sin = jnp.where(live, jnp.sin(ang), 0.0)
    fwd = jnp.concatenate([cos, -sin], axis=0)
    wgt = jnp.where((f == 0) | (f == bk), 1.0, 2.0) / (2 * bk)
    inv = jnp.concatenate([cos * wgt, -sin * wgt], axis=0).T
    return fwd.astype(BF16), inv.astype(BF16), nfp


def _spectra_kernel(k_ref, nrm_ref, f_ref, c_ref, a_prev, k_prev, *, nfp):
    s = pl.program_id(2)
    kb = k_ref[0]
    a = jnp.dot(f_ref[...], kb.astype(BF16), preferred_element_type=F32)

    @pl.when(s > 0)
    def _():
        row = lax.broadcasted_iota(jnp.int32, a.shape, 0)
        sign = jnp.where((row & 1) == 0, 1.0, -1.0)
        k0 = jnp.where(row < nfp, k_prev[0:1, :], 0.0)
        c = (a + sign * (a_prev[...] - k0)) / nrm_ref[0, 0:1, :]
        width = c_ref.shape[-1]
        for t in range(c_ref.shape[1]):
            c_ref[0, t, 0] = c[:, t * width:(t + 1) * width]

    a_prev[...] = a
    k_prev[...] = kb[0:SUBLANES, :]


def _hyena_spectra(kappa, nrm, fwd, bk, nfp, dt_out, dt=256):
    order, two_l, D = kappa.shape
    n_blk = two_l // bk
    dt = max(_tile(D, dt), dt_out)
    per_step = dt // dt_out
    return pl.pallas_call(
        functools.partial(_spectra_kernel, nfp=nfp),
        grid=(order, D // dt, n_blk),
        in_specs=[pl.BlockSpec((1, bk, dt), lambda n, d, s: (n, s, d)),
                  pl.BlockSpec((1, SUBLANES, dt), lambda n, d, s: (n, 0, d)),
                  pl.BlockSpec((2 * nfp, bk), lambda n, d, s: (0, 0))],
        out_specs=pl.BlockSpec((1, per_step, 1, 2 * nfp, dt_out),
                               lambda n, d, s: (n, d, jnp.maximum(s - 1, 0), 0, 0)),
        out_shape=jax.ShapeDtypeStruct((order, D // dt_out, n_blk - 1, 2 * nfp, dt_out), F32),
        scratch_shapes=[pltpu.VMEM((2 * nfp, dt), F32), pltpu.VMEM((SUBLANES, dt), F32)],
        compiler_params=_params(("arbitrary", "arbitrary", "arbitrary")),
        name="hyena_spectra",
    )(kappa, nrm, fwd)


def _short_conv_block(src_ref, w_ref, b_ref, j, nb, bk):
    cur = src_ref[0, j * bk:(j + 1) * bk, :]
    row = lax.broadcasted_iota(jnp.int32, cur.shape, 0)
    if j > 0:
        before = src_ref[0, j * bk - SUBLANES:j * bk, :][SUBLANES - 1:SUBLANES, :]
    else:
        before = jnp.zeros_like(cur[0:1, :])
    if j < nb - 1:
        after = src_ref[0, (j + 1) * bk:(j + 1) * bk + SUBLANES, :][0:1, :]
    else:
        after = jnp.zeros_like(cur[0:1, :])
    up = jnp.where(row == 0, before, pltpu.roll(cur, 1, 0))
    dn = jnp.where(row == bk - 1, after, pltpu.roll(cur, bk - 1, 0))
    return b_ref[...] + up * w_ref[0:1, :] + cur * w_ref[1:2, :] + dn * w_ref[2:3, :]


def _long_conv_kernel(*refs, nb, bk, nfp, conv_input):
    if conv_input:
        (z_ref, zw_ref, zb_ref, g_ref, gw_ref, gb_ref, skip_ref, c_ref, fwd_ref, inv_ref,
         o_ref, zf_ref, y_ref) = refs
    else:
        (z_ref, g_ref, gw_ref, gb_ref, skip_ref, c_ref, fwd_ref, inv_ref, o_ref, zf_ref, y_ref) = refs

    def z_block(j):
        if conv_input:
            return _short_conv_block(z_ref, zw_ref, zb_ref, j, nb, bk)
        return z_ref[0, j * bk:(j + 1) * bk, :]

    for j in range(nb):
        zf_ref[j] = jnp.dot(fwd_ref[...], z_block(j).astype(BF16), preferred_element_type=F32)

    for i in range(nb):
        def chunk(c, carry, i=i):
            r0 = pl.multiple_of(c * SUBLANES, SUBLANES)
            re_rows, im_rows = pl.ds(r0, SUBLANES), pl.ds(nfp + r0, SUBLANES)
            acc_re = acc_im = None
            for j in range(nb):
                m = i - j + nb - 1
                c_re, c_im = c_ref[0, 0, m, re_rows, :], c_ref[0, 0, m, im_rows, :]
                z_re, z_im = zf_ref[j, re_rows, :], zf_ref[j, im_rows, :]
                t_re = c_re * z_re - c_im * z_im
                t_im = c_re * z_im + c_im * z_re
                acc_re = t_re if acc_re is None else acc_re + t_re
                acc_im = t_im if acc_im is None else acc_im + t_im
            y_ref[re_rows, :] = acc_re
            y_ref[im_rows, :] = acc_im
            return carry

        lax.fori_loop(0, nfp // SUBLANES, chunk, 0, unroll=5 if nb >= 8 else 2)
        y = jnp.dot(inv_ref[...], y_ref[...].astype(BF16), preferred_element_type=F32)
        gate = _short_conv_block(g_ref, gw_ref, gb_ref, i, nb, bk)
        o_ref[0, i * bk:(i + 1) * bk, :] = (gate * (y + skip_ref[...] * z_block(i))).astype(o_ref.dtype)


def _long_conv(u3, z3, order, spectra, fwd, inv, conv_w, conv_b, skip, *, b0, nbatch, bk, nfp, dt, out_dtype):
    _, L, three_d = u3.shape
    D = three_d // 3
    nb = L // bk
    n_d = D // dt
    conv_input = z3 is None
    slab = lambda k: (lambda d, b: (b0 + b, 0, k * n_d + d))
    cols = lambda k: (lambda d, b: (0, k * n_d + d))
    seq_block = (1, L, dt)
    own = lambda d, b: (b, 0, d)
    in_specs, args = [], []
    if conv_input:
        in_specs += [pl.BlockSpec(seq_block, slab(0)), pl.BlockSpec((SHORT_CONV, dt), cols(0)),
                     pl.BlockSpec((1, dt), cols(0))]
        args += [u3, conv_w, conv_b]
    else:
        in_specs += [pl.BlockSpec(seq_block, own)]
        args += [z3]
    in_specs += [pl.BlockSpec(seq_block, slab(order + 1)), pl.BlockSpec((SHORT_CONV, dt), cols(order + 1)),
                 pl.BlockSpec((1, dt), cols(order + 1)),
                 pl.BlockSpec((1, dt), lambda d, b: (0, d)),
                 pl.BlockSpec((1, 1, 2 * nb - 1, 2 * nfp, dt), lambda d, b: (order, d, 0, 0, 0),
                              pipeline_mode=pl.Buffered(1)),
                 pl.BlockSpec((2 * nfp, bk), lambda d, b: (0, 0)),
                 pl.BlockSpec((bk, 2 * nfp), lambda d, b: (0, 0))]
    args += [u3, conv_w, conv_b, skip[order:order + 1], spectra, fwd, inv]
    return pl.pallas_call(
        functools.partial(_long_conv_kernel, nb=nb, bk=bk, nfp=nfp, conv_input=conv_input),
        grid=(n_d, nbatch),
        in_specs=in_specs,
        out_specs=pl.BlockSpec(seq_block, own),
        out_shape=jax.ShapeDtypeStruct((nbatch, L, D), out_dtype),
        scratch_shapes=[pltpu.VMEM((nb, 2 * nfp, dt), F32), pltpu.VMEM((2 * nfp, dt), F32)],
        compiler_params=_params(("arbitrary", "arbitrary")),
        name="hyena_long_conv",
    )(*args)


def _hyena_mixer(x_parts, groups, p):
    T, D = sum(x.shape[0] for x in x_parts), x_parts[0].shape[1]
    u = _project(x_parts, p["hy_w_in"].astype(BF16), p["hy_b_in"], F32)
    conv_w, conv_b, skip = p["hy_conv_w"], p["hy_conv_b"].reshape(1, 3 * D), p["hy_filt_skip"]
    outs = []
    for first, nbatch, L in groups:
        bk = min(L, HYENA_BLOCK)
        dt = _tile(D, HYENA_CHANNEL_TILE if L <= HYENA_LONG_SEQ else HYENA_CHANNEL_TILE // 2)
        fwd, inv, nfp = _dft_matrices(bk)
        kappa, nrm = _hyena_filters(L, D, p["hy_filt_w_in"], p["hy_filt_b_in"], p["hy_filt_w_mid"],
                                    p["hy_filt_b_mid"], p["hy_filt_freq"], p["hy_filt_w_out"])
        spectra = _hyena_spectra(kappa, nrm, fwd, bk, nfp, dt)
        kw = dict(b0=first // L, nbatch=nbatch, bk=bk, nfp=nfp, dt=dt)
        u3 = u.reshape(T // L, L, 3 * D)
        z1 = _long_conv(u3, None, 0, spectra, fwd, inv, conv_w, conv_b, skip, out_dtype=F32, **kw)
        z2 = _long_conv(u3, z1, 1, spectra, fwd, inv, conv_w, conv_b, skip, out_dtype=BF16, **kw)
        outs.append(z2.reshape(nbatch * L, D))
    return outs


def _attn_kernel(q_ref, k_ref, v_ref, lam_ref, g_ref, o_ref, *, lam_init, kc):
    dh = DIFF_HEAD_DIM
    lam_dots = jnp.sum(lam_ref[0:2, :] * lam_ref[2:4, :], axis=-1, keepdims=True)
    lam = jnp.exp(lam_dots[0:1, :]) - jnp.exp(lam_dots[1:2, :]) + lam_init
    q = q_ref[...]
    n_chunks = k_ref.shape[0] // kc
    state = [None, None]
    for j in range(n_chunks):
        rows = slice(j * kc, (j + 1) * kc)
        v = v_ref[rows, :]
        for c in range(2):
            s = lax.dot_general(q[:, c * dh:(c + 1) * dh], k_ref[rows, c * dh:(c + 1) * dh],
                                (((1,), (1,)), ((), ())), preferred_element_type=F32)
            top = jnp.max(s, axis=-1, keepdims=True)
            if state[c] is None:
                e = jnp.exp2(s - top)
                state[c] = (top, jnp.sum(e, axis=-1, keepdims=True),
                            jnp.dot(e.astype(BF16), v, preferred_element_type=F32))
            else:
                m_old, l_old, acc = state[c]
                m_new = jnp.maximum(m_old, top)
                shrink = jnp.exp2(m_old - m_new)
                e = jnp.exp2(s - m_new)
                state[c] = (m_new, shrink * l_old + jnp.sum(e, axis=-1, keepdims=True),
                            shrink * acc + jnp.dot(e.astype(BF16), v, preferred_element_type=F32))
    (_, l0, o0), (_, l1, o1) = state
    o = o0 * (1.0 / l0) - o1 * (lam / l1)
    o = o * lax.rsqrt(jnp.mean(o * o, axis=-1, keepdims=True) + LN_EPS) * g_ref[...]
    o_ref[...] = (o * (1.0 - lam_init)).astype(o_ref.dtype)


def _diff_attention(qkv, lam_vecs, subln_g, layer_idx, *, first, nbatch, L, tq, kc):
    T, three_d = qkv.shape
    D = three_d // 3
    hw = 2 * DIFF_HEAD_DIM
    H = D // hw
    tq, kc = _tile(L, tq), _tile(L, kc)
    nq = L // tq
    q0, b0 = first // tq, first // L
    lam_init = 0.8 - 0.6 * math.exp(-0.3 * layer_idx)
    return pl.pallas_call(
        functools.partial(_attn_kernel, lam_init=lam_init, kc=kc),
        grid=(nbatch, H, nq),
        in_specs=[pl.BlockSpec((tq, hw), lambda b, h, i: (q0 + b * nq + i, h)),
                  pl.BlockSpec((L, hw), lambda b, h, i: (b0 + b, H + h)),
                  pl.BlockSpec((L, hw), lambda b, h, i: (b0 + b, 2 * H + h)),
                  pl.BlockSpec((4, DIFF_HEAD_DIM), lambda b, h, i: (0, 0)),
                  pl.BlockSpec((1, hw), lambda b, h, i: (0, 0))],
        out_specs=pl.BlockSpec((tq, hw), lambda b, h, i: (b * nq + i, h)),
        out_shape=jax.ShapeDtypeStruct((nbatch * L, D), BF16),
        compiler_params=_params(("arbitrary", "arbitrary", "arbitrary")),
        name="diff_attention",
    )(qkv, qkv, qkv, lam_vecs, subln_g.reshape(1, hw))


def _attention_mixer(x, groups, p, layer_idx):
    split = groups[1][0] if len(groups) > 1 else x.shape[0]
    l1, l2 = groups[0][2], groups[-1][2]
    qkv = _qkv_project(x, p["da_w_qkv"].astype(BF16), split, l1, l2)
    lam_vecs = jnp.stack([p["da_lambda_q1"], p["da_lambda_q2"], p["da_lambda_k1"], p["da_lambda_k2"]])
    return [_diff_attention(qkv, lam_vecs, p["da_subln_g"], layer_idx, first=first, nbatch=nbatch, L=L,
                            tq=ATTN_Q_TILE * (1 if L > 2048 else 2), kc=ATTN_KEY_CHUNK // (1 if L > 2048 else 2))
            for first, nbatch, L in groups]


def _row_copy(src, src_row, dst, dst_row, sem):
    return pltpu.make_async_copy(src.at[pl.ds(src_row, 1)], dst.at[pl.ds(dst_row, 1)], sem)


def _dispatch_kernel(pend_ref, padded_ref, dest_ref, x_ref, xs_out, zeros, sem, zero_sem, *, tm, rb, n_experts):
    @pl.when(pl.program_id(0) == 0)
    def _():
        zeros[...] = jnp.zeros_like(zeros)

        def last_tile(e):
            first_row = pl.multiple_of(pend_ref[e] - rb, rb)
            return pltpu.make_async_copy(zeros, xs_out.at[pl.ds(first_row, rb)], zero_sem)

        for e in range(n_experts):
            @pl.when(padded_ref[e] > 0)
            def _(e=e):
                last_tile(e).start()

        for e in range(n_experts):
            @pl.when(padded_ref[e] > 0)
            def _(e=e):
                last_tile(e).wait()

    def issue(group, carry):
        base = pl.multiple_of(group * SUBLANES, SUBLANES)
        for u in range(SUBLANES):
            for k in range(TOP_K):
                _row_copy(x_ref, base + u, xs_out, dest_ref[(base + u) * TOP_K + k], sem).start(priority=k % 2)
        return carry

    lax.fori_loop(0, tm // SUBLANES, issue, 0)
    done = xs_out.at[pl.ds(0, TOP_K * tm)]
    pltpu.make_async_copy(done, done, sem).wait()


def _dispatch(x_packed, dest_flat, padded_end, padded, n_rows, rb, tm=512):
    T, half = x_packed.shape
    tm = _tile(T, tm)
    return pl.pallas_call(
        functools.partial(_dispatch_kernel, tm=tm, rb=rb, n_experts=padded.shape[0]),
        grid_spec=pltpu.PrefetchScalarGridSpec(
            num_scalar_prefetch=2,
            grid=(T // tm,),
            in_specs=[pl.BlockSpec((tm * TOP_K,), lambda i, pe, pd: (i,), memory_space=pltpu.SMEM),
                      pl.BlockSpec((tm, half), lambda i, pe, pd: (i, 0))],
            out_specs=pl.BlockSpec(memory_space=pl.ANY),
            scratch_shapes=[pltpu.VMEM((rb, half), jnp.uint32), pltpu.SemaphoreType.DMA(()),
                            pltpu.SemaphoreType.DMA(())]),
        out_shape=jax.ShapeDtypeStruct((n_rows, half), jnp.uint32),
        compiler_params=_params(("arbitrary",)),
        name="moe_dispatch",
    )(padded_end, padded, dest_flat, x_packed)


def _expert_kernel(tile_expert_ref, n_used_ref, x_ref, bg_ref, bu_ref, bd_ref, wg_hbm, wu_hbm, wd_hbm, o_ref,
                   wg, wu, wd, stage, sem, *, chunk):
    i = pl.program_id(0)
    expert = tile_expert_ref[i]

    @pl.when(i < n_used_ref[0])
    def _():
        @pl.when((i == 0) | (tile_expert_ref[jnp.maximum(i - 1, 0)] != expert))
        def _():
            pieces = [(src, dst, r) for src, dst in ((wg_hbm, wg), (wu_hbm, wu), (wd_hbm, wd))
                      for r in range(0, dst.shape[0], chunk)]

            def fetch(n):
                src, _, r = pieces[n]
                return pltpu.make_async_copy(src.at[expert, pl.ds(r, chunk)], stage.at[n % 2], sem.at[n % 2])

            fetch(0).start()
            for n, (_, dst, r) in enumerate(pieces):
                if n + 1 < len(pieces):
                    fetch(n + 1).start()
                fetch(n).wait()
                dst[r:r + chunk, :] = stage[n % 2].astype(BF16)

        x = _unpack_bf16_pairs(x_ref[...])
        gate = jnp.dot(x, wg[...], preferred_element_type=F32) + bg_ref[0]
        up = jnp.dot(x, wu[...], preferred_element_type=F32) + bu_ref[0]
        gate = jnp.minimum(gate, SWIGLU_LIMIT)
        up = jnp.clip(up, -SWIGLU_LIMIT, SWIGLU_LIMIT)
        h = gate * (1.0 / (1.0 + jnp.exp(-SWIGLU_ALPHA * gate))) * (up + 1.0)
        y = jnp.dot(h.astype(BF16), wd[...], preferred_element_type=F32) + bd_ref[0]
        o_ref[...] = _pack_bf16_pairs(y)


def _experts(xs, tile_expert, n_used, wg, bg, wu, bu, wd, bd, rb, chunk=512):
    P, half = xs.shape
    E, D, F = wg.shape
    assert D == F, "one staging buffer shape serves all three weight matrices"
    n_tiles = P // rb
    chunk = _tile(D, chunk)
    of_expert = lambda i, te, nu: (te[i], 0, 0)
    hbm = pl.BlockSpec(memory_space=pl.ANY)
    return pl.pallas_call(
        functools.partial(_expert_kernel, chunk=chunk),
        grid_spec=pltpu.PrefetchScalarGridSpec(
            num_scalar_prefetch=2,
            grid=(n_tiles,),
            in_specs=[pl.BlockSpec((rb, half), lambda i, te, nu: (jnp.minimum(i, nu[0] - 1), 0)),
                      pl.BlockSpec((1, 1, F), of_expert), pl.BlockSpec((1, 1, F), of_expert),
                      pl.BlockSpec((1, 1, D), of_expert), hbm, hbm, hbm],
            out_specs=pl.BlockSpec((rb, half), lambda i, te, nu: (i, 0)),
            scratch_shapes=[pltpu.VMEM((D, F), BF16), pltpu.VMEM((D, F), BF16), pltpu.VMEM((F, D), BF16),
                            pltpu.VMEM((2, chunk, F), F32), pltpu.SemaphoreType.DMA((2,))]),
        out_shape=jax.ShapeDtypeStruct((P, half), jnp.uint32),
        compiler_params=_params(("arbitrary",)),
        name="moe_experts",
    )(tile_expert, n_used, xs, bg, bu, bd, wg, wu, wd)


def _combine_kernel(dest_ref, dest_next_ref, gate_ref, x_ref, g_ref, beta_ref, ys_hbm, *rest,
                    tc, n_steps, n_first):
    *o_refs, rows, sem = rest
    i = pl.program_id(0)
    slot = lax.rem(i, 2)

    def gather(d_ref, s):
        def issue(group, carry):
            base = pl.multiple_of(group * SUBLANES, SUBLANES)
            for u in range(SUBLANES):
                for k in range(TOP_K):
                    _row_copy(ys_hbm, d_ref[(base + u) * TOP_K + k], rows.at[s, k], base + u,
                              sem.at[s]).start(priority=k % 2)
            return carry
        lax.fori_loop(0, tc // SUBLANES, issue, 0)

    @pl.when(i == 0)
    def _():
        gather(dest_ref, 0)

    @pl.when(i + 1 < n_steps)
    def _():
        gather(dest_next_ref, 1 - slot)

    pltpu.make_async_copy(rows.at[slot], rows.at[slot], sem.at[slot]).wait()
    gates = gate_ref[...]
    mix_hi = mix_lo = None
    for k in range(TOP_K):
        w = rows[slot, k]
        hi = gates[:, k:k + 1] * pltpu.bitcast(w & jnp.uint32(0xFFFF0000), F32)
        lo = gates[:, k:k + 1] * pltpu.bitcast(w << 16, F32)
        mix_hi = hi if mix_hi is None else mix_hi + hi
        mix_lo = lo if mix_lo is None else mix_lo + lo
    mix = jnp.concatenate([mix_hi, mix_lo], axis=1)
    y = _layer_norm(ALPHA * x_ref[...] + mix, g_ref[...], beta_ref[...])
    if n_first is None:
        o_refs[0][...] = y
    else:
        @pl.when(i < n_first)
        def _():
            o_refs[0][...] = y

        @pl.when(i >= n_first)
        def _():
            o_refs[1][...] = y


def _combine_ln(ys, dest_flat, gates, x, g, beta, split=None, tc=128):
    T, D = x.shape
    tc = _tile(T if split is None else math.gcd(split, T - split), tc)
    n_steps = T // tc
    row = lambda i: (i, 0)
    fixed = lambda i: (0, 0)
    tile = (tc, D)
    if split is None:
        n_first = None
        out_specs = pl.BlockSpec(tile, row)
        out_shape = jax.ShapeDtypeStruct((T, D), F32)
    else:
        n_first = split // tc
        out_specs = [pl.BlockSpec(tile, lambda i: (jnp.minimum(i, n_first - 1), 0)),
                     pl.BlockSpec(tile, lambda i: (jnp.maximum(i - n_first, 0), 0))]
        out_shape = [jax.ShapeDtypeStruct((split, D), F32), jax.ShapeDtypeStruct((T - split, D), F32)]
    return pl.pallas_call(
        functools.partial(_combine_kernel, tc=tc, n_steps=n_steps, n_first=n_first),
        grid=(n_steps,),
        in_specs=[pl.BlockSpec((tc * TOP_K,), lambda i: (i,), memory_space=pltpu.SMEM),
                  pl.BlockSpec((tc * TOP_K,), lambda i: (jnp.minimum(i + 1, n_steps - 1),),
                               memory_space=pltpu.SMEM),
                  pl.BlockSpec((tc, LANES), row), pl.BlockSpec(tile, row),
                  pl.BlockSpec((1, D), fixed), pl.BlockSpec((1, D), fixed),
                  pl.BlockSpec(memory_space=pl.ANY)],
        out_specs=out_specs,
        out_shape=out_shape,
        scratch_shapes=[pltpu.VMEM((2, TOP_K, tc, D // 2), jnp.uint32), pltpu.SemaphoreType.DMA((2,))],
        compiler_params=_params(("arbitrary",)),
        name="moe_combine_ln",
    )(dest_flat, dest_flat, gates, x, g.reshape(1, D), beta.reshape(1, D), ys)


def _expert_stacks(p):
    n_layers, E, D, F = p["moe_w_gate"].shape
    stack = lambda w: w.reshape((n_layers * E,) + w.shape[2:])
    bias = lambda b: b.reshape(n_layers * E, 1, b.shape[-1])
    return (stack(p["moe_w_gate"]), bias(p["moe_b_gate"]), stack(p["moe_w_up"]), bias(p["moe_b_up"]),
            stack(p["moe_w_down"]), bias(p["moe_b_down"]))


def _moe_ln(x, x_packed, routing, p, stacks, i, rb, split=None):
    T, D = x.shape
    E = p["moe_router_w"].shape[-1]
    idx, gates, rank, counts = routing
    counts = counts[0, :E].astype(jnp.int32)
    padded = (counts + rb - 1) // rb * rb
    padded_end = jnp.cumsum(padded)
    padded_start = padded_end - padded
    n_tiles = (T * TOP_K) // rb + E
    tile_row = jnp.arange(n_tiles, dtype=jnp.int32) * rb
    tile_expert = jnp.minimum(jnp.sum(padded_end[None, :] <= tile_row[:, None], axis=1), E - 1).astype(jnp.int32)
    n_used = (padded_end[-1:] // rb).astype(jnp.int32)
    dest = (padded_start[idx[:, :TOP_K]] + rank[:, :TOP_K]).reshape(T * TOP_K)
    xs = _dispatch(x_packed, dest, padded_end.astype(jnp.int32), padded, n_tiles * rb, rb)
    ys = _experts(xs, tile_expert + i * E, n_used, *stacks, rb)
    return _combine_ln(ys, dest, gates, x, p["ln2_g"][i], p["ln2_b"][i], split)


def _encoder(x_parts, groups, p, rb):
    stacks = _expert_stacks(p)
    D = x_parts[0].shape[1]
    for i in range(DEPTH):
        j = i // 2
        if i % 2 == 0:
            pj = {k: v[j] for k, v in p.items() if k.startswith("hy_")}
            a_parts = _hyena_mixer(x_parts, groups, pj)
            w_o, b_o = pj["hy_w_out"], pj["hy_b_out"]
        else:
            pj = {k: v[j] for k, v in p.items() if k.startswith("da_")}
            a_parts = _attention_mixer(x_parts[0], groups, pj, i)
            w_o, b_o = pj["da_w_o"], jnp.zeros((D,), F32)
        x, x_packed, *routing = _out_project_ln_route(a_parts, w_o.astype(BF16), b_o, x_parts, p["ln1_g"][i],
                                                      p["ln1_b"][i], p["moe_router_w"][i], p["moe_router_b"][i])
        last = i == DEPTH - 1
        y = _moe_ln(x, x_packed, routing, p, stacks, i, rb, groups[1][0] if last and len(groups) > 1 else None)
        x_parts = list(y) if isinstance(y, (list, tuple)) else [y]
    return x_parts


def _forward(x_prompt, x_sample, p, rb=512):
    B1, L1, D = x_prompt.shape
    B2, L2, _ = x_sample.shape
    T1, T2 = B1 * L1, B2 * L2
    assert T1 % L2 == 0, "sample sequences must start on a multiple of their length"
    y1, y2 = _encoder([x_prompt.reshape(T1, D), x_sample.reshape(T2, D)], [(0, B1, L1), (T1, B2, L2)], p, rb)
    return y1.reshape(B1, L1, D), y2.reshape(B2, L2, D)


def kernel(x_prompt, x_sample, hy_w_in, hy_b_in, hy_conv_w, hy_conv_b, hy_filt_w_in, hy_filt_b_in, hy_filt_w_mid, hy_filt_b_mid, hy_filt_freq, hy_filt_w_out, hy_filt_skip, hy_w_out, hy_b_out, da_w_qkv, da_lambda_q1, da_lambda_k1, da_lambda_q2, da_lambda_k2, da_subln_g, da_w_o, ln1_g, ln1_b, moe_router_w, moe_router_b, moe_w_gate, moe_b_gate, moe_w_up, moe_b_up, moe_w_down, moe_b_down, ln2_g, ln2_b):
    p = dict(hy_w_in=hy_w_in, hy_b_in=hy_b_in, hy_conv_w=hy_conv_w, hy_conv_b=hy_conv_b,
             hy_filt_w_in=hy_filt_w_in, hy_filt_b_in=hy_filt_b_in, hy_filt_w_mid=hy_filt_w_mid,
             hy_filt_b_mid=hy_filt_b_mid, hy_filt_freq=hy_filt_freq, hy_filt_w_out=hy_filt_w_out,
             hy_filt_skip=hy_filt_skip, hy_w_out=hy_w_out, hy_b_out=hy_b_out,
             da_w_qkv=da_w_qkv, da_lambda_q1=da_lambda_q1, da_lambda_k1=da_lambda_k1,
             da_lambda_q2=da_lambda_q2, da_lambda_k2=da_lambda_k2, da_subln_g=da_subln_g, da_w_o=da_w_o,
             ln1_g=ln1_g, ln1_b=ln1_b, moe_router_w=moe_router_w, moe_router_b=moe_router_b,
             moe_w_gate=moe_w_gate, moe_b_gate=moe_b_gate, moe_w_up=moe_w_up, moe_b_up=moe_b_up,
             moe_w_down=moe_w_down, moe_b_down=moe_b_down, ln2_g=ln2_g, ln2_b=ln2_b)
    return _forward(x_prompt, x_sample, p)
```

```python
import functools
import math

import jax
import jax.numpy as jnp
from jax import lax
from jax.experimental import pallas as pl
from jax.experimental.pallas import tpu as pltpu

DEPTH = 2
SHORT_CONV = 3
HYENA_ORDER = 2
FILTER_EMB = 33
FILTER_BANDS = (FILTER_EMB - 1) // 2
FILTER_INNER = 2
FAST_DECAY_PCT = 0.3
SLOW_DECAY_PCT = 1.5
DECAY_TARGET = 1e-2
MAX_DECAY = math.log(DECAY_TARGET) / FAST_DECAY_PCT
MIN_DECAY = math.log(DECAY_TARGET) / SLOW_DECAY_PCT
DIFF_HEAD_DIM = 128
ROPE_DIM = DIFF_HEAD_DIM // 4
ROPE_THETA = 500000.0
TOP_K = 4
SWIGLU_ALPHA = 1.702
SWIGLU_LIMIT = 7.0
LN_EPS = 1e-5
ALPHA = (2 * DEPTH) ** 0.25

LANES = 128
SUBLANES = 8
VMEM_LIMIT_BYTES = 60 * 1024 * 1024

ROW_TILE = 512
PROJ_COL_TILE = 2048
EXPERT_ROW_TILE = 512
EXPERT_STAGE_ROWS = 512
DISPATCH_TOKENS = 512
COMBINE_TOKENS = 256
FILTER_ROW_TILE = 1024

HYENA_BLOCK = 512
HYENA_CHANNEL_TILE = 256
HYENA_LONG_SEQ = 2048

ATTN_Q_TILE = 512
ATTN_KEY_CHUNK = 2048

F32 = jnp.float32
BF16 = jnp.bfloat16


def _params(semantics, vmem=VMEM_LIMIT_BYTES):
    return pltpu.CompilerParams(dimension_semantics=semantics, vmem_limit_bytes=vmem)


def _tile(n, pref):
    t = min(n, pref)
    assert n % t == 0, (n, t)
    return t


def _split_bf16(a):
    hi = a.astype(BF16)
    lo = (a - hi.astype(F32)).astype(BF16)
    return hi, lo


def _dot3(a, b):
    ah, al = _split_bf16(a)
    bh, bl = _split_bf16(b)
    d = functools.partial(jnp.dot, preferred_element_type=F32)
    return d(ah, bh) + d(al, bh) + d(ah, bl)


def _layer_norm(y, g, b):
    mu = jnp.mean(y, axis=-1, keepdims=True)
    yc = y - mu
    var = jnp.mean(yc * yc, axis=-1, keepdims=True)
    return yc * lax.rsqrt(var + LN_EPS) * g + b


def _row_sources(arrays, tm):
    specs, starts, lo = [], [], 0
    for arr in arrays:
        n = arr.shape[0] // tm
        assert n * tm == arr.shape[0], (arr.shape, tm)
        specs.append(pl.BlockSpec((tm, arr.shape[1]), lambda *idx, lo=lo, n=n: (jnp.clip(idx[-1] - lo, 0, n - 1), 0)))
        starts.append(lo)
        lo += n
    return specs, tuple(starts)


def _select_rows(refs, starts, i):
    rows = refs[0][...]
    for ref, lo in zip(refs[1:], starts[1:]):
        rows = jnp.where(i >= lo, ref[...], rows)
    return rows


def _proj_kernel(*refs, starts):
    *x_refs, w_ref, b_ref, o_ref = refs
    x = _select_rows(x_refs, starts, pl.program_id(1))
    acc = jnp.dot(x.astype(BF16), w_ref[...], preferred_element_type=F32)
    o_ref[...] = (acc + b_ref[...]).astype(o_ref.dtype)


def _project(xs, w, b, out_dtype, tm=ROW_TILE, tn=PROJ_COL_TILE):
    T = sum(x.shape[0] for x in xs)
    K, N = w.shape
    tm, tn = _tile(math.gcd(*[x.shape[0] for x in xs]), tm), _tile(N, tn)
    x_specs, starts = _row_sources(xs, tm)
    return pl.pallas_call(
        functools.partial(_proj_kernel, starts=starts),
        grid=(N // tn, T // tm),
        in_specs=x_specs + [pl.BlockSpec((K, tn), lambda n, m: (0, n)),
                            pl.BlockSpec((1, tn), lambda n, m: (0, n))],
        out_specs=pl.BlockSpec((tm, tn), lambda n, m: (m, n)),
        out_shape=jax.ShapeDtypeStruct((T, N), out_dtype),
        compiler_params=_params(("arbitrary", "arbitrary")),
        name="project",
    )(*xs, w, b.reshape(1, N))


def _qkv_kernel(x_ref, w_ref, rope_ref, o_ref, *, n_rope_slabs):
    n = pl.program_id(0)
    acc = jnp.dot(x_ref[...].astype(BF16), w_ref[...], preferred_element_type=F32)
    tn = acc.shape[1]

    @pl.when(n >= n_rope_slabs)
    def _():
        o_ref[...] = acc.astype(o_ref.dtype)

    @pl.when(n < n_rope_slabs)
    def _():
        scale = jnp.where(n == 0, DIFF_HEAD_DIM ** -0.5 * math.log2(math.e), 1.0)
        c_tab, s_lo, s_hi = rope_ref[0] * scale, rope_ref[1] * scale, rope_ref[2] * scale
        half = ROPE_DIM // 2
        for h in range(tn // LANES):
            xh = acc[:, h * LANES:(h + 1) * LANES]
            rot = (xh * c_tab + pltpu.roll(xh, LANES - half, 1) * s_lo
                   + pltpu.roll(xh, half, 1) * s_hi)
            o_ref[:, h * LANES:(h + 1) * LANES] = rot.astype(o_ref.dtype)


def _rope_tables(n_pos):
    half = ROPE_DIM // 2
    inv_freq = ROPE_THETA ** (-jnp.arange(0, ROPE_DIM, 2, dtype=F32) / ROPE_DIM)
    ang = jnp.arange(n_pos, dtype=F32)[:, None] * inv_freq[None, :]
    cos, sin = jnp.cos(ang), jnp.sin(ang)
    rest = LANES - ROPE_DIM
    pad = lambda parts, fill: jnp.concatenate(parts + [jnp.full((n_pos, rest), fill, F32)], axis=1)
    zero = jnp.zeros_like(sin)
    return jnp.stack([pad([cos, cos], 1.0), pad([-sin, zero], 0.0), pad([zero, sin], 0.0)])


def _qkv_project(x, w, split, l1, l2, tm=ROW_TILE):
    T, K = x.shape
    N = w.shape[1]
    tn = N // 3
    tm = _tile(math.gcd(l1, l2), tm)

    def pos_block(m):
        start = m * tm
        pos0 = jnp.where(start < split, lax.rem(start, l1), lax.rem(jnp.maximum(start - split, 0), l2))
        return pos0 // tm

    return pl.pallas_call(
        functools.partial(_qkv_kernel, n_rope_slabs=2),
        grid=(3, T // tm),
        in_specs=[pl.BlockSpec((tm, K), lambda n, m: (m, 0)),
                  pl.BlockSpec((K, tn), lambda n, m: (0, n)),
                  pl.BlockSpec((3, tm, LANES), lambda n, m: (0, pos_block(m), 0))],
        out_specs=pl.BlockSpec((tm, tn), lambda n, m: (m, n)),
        out_shape=jax.ShapeDtypeStruct((T, N), BF16),
        compiler_params=_params(("arbitrary", "arbitrary")),
        name="qkv_project",
    )(x, w, _rope_tables(max(l1, l2)))


def _pack_bf16_pairs(y):
    half = y.shape[1] // 2
    r = pltpu.bitcast(y.astype(BF16).astype(F32), jnp.uint32)
    return (r[:, :half] & jnp.uint32(0xFFFF0000)) | (r[:, half:] >> 16)


def _unpack_bf16_pairs(w):
    hi = pltpu.bitcast(w & jnp.uint32(0xFFFF0000), F32)
    lo = pltpu.bitcast(w << 16, F32)
    return jnp.concatenate([hi, lo], axis=1).astype(BF16)


def _route_rows(x, w_ref, b_ref, idx_ref, gate_ref, rank_ref, cnt_ref, base_ref, n_experts):
    @pl.when(pl.program_id(0) == 0)
    def _():
        base_ref[...] = jnp.zeros_like(base_ref)

    logits = _dot3(x, w_ref[...]) + b_ref[...]
    tm = logits.shape[0]
    lane = lax.broadcasted_iota(jnp.int32, (tm, LANES), 1)
    neg_inf = jnp.float32(-jnp.inf)
    lane_f = lane.astype(F32)
    live = jnp.where(lane < n_experts, logits, neg_inf)
    vals, picks, hots = [], [], []
    for _ in range(TOP_K):
        best = jnp.max(live, axis=-1, keepdims=True)
        pick = jnp.min(jnp.where(live == best, lane_f, float(LANES)), axis=-1, keepdims=True)
        hot = lane_f == pick
        live = jnp.where(hot, neg_inf, live)
        vals.append(best)
        picks.append(pick.astype(jnp.int32))
        hots.append(hot)
    exps = [jnp.exp(v - vals[0]) for v in vals]
    denom = exps[0] + exps[1] + exps[2] + exps[3]
    chosen = jnp.zeros((tm, LANES), F32)
    for hot in hots:
        chosen = chosen + hot.astype(F32)
    r_i = lax.broadcasted_iota(jnp.int32, (tm, tm), 0)
    c_i = lax.broadcasted_iota(jnp.int32, (tm, tm), 1)
    earlier = jnp.where(r_i > c_i, 1.0, 0.0).astype(BF16)
    before = base_ref[0:1, :] + jnp.dot(earlier, chosen.astype(BF16), preferred_element_type=F32)
    idx_out = jnp.zeros((tm, LANES), jnp.int32)
    rank_out = jnp.zeros((tm, LANES), jnp.int32)
    gate_out = jnp.zeros((tm, LANES), F32)
    for k in range(TOP_K):
        rank_k = jnp.sum(jnp.where(hots[k], before, 0.0), axis=-1, keepdims=True).astype(jnp.int32)
        idx_out = jnp.where(lane == k, picks[k], idx_out)
        rank_out = jnp.where(lane == k, rank_k, rank_out)
        gate_out = jnp.where(lane == k, exps[k] / denom, gate_out)
    idx_ref[...] = idx_out
    rank_ref[...] = rank_out
    gate_ref[...] = gate_out
    base_ref[...] = base_ref[...] + jnp.sum(chosen, axis=0, keepdims=True)
    cnt_ref[...] = base_ref[...]


def _out_ln_route_kernel(*refs, n_a, n_x, a_starts, x_starts, n_experts):
    a_refs, refs = refs[:n_a], refs[n_a:]
    (w_ref, b_ref), refs = refs[:2], refs[2:]
    x_refs, refs = refs[:n_x], refs[n_x:]
    g_ref, beta_ref, rw_ref, rb_ref, o_ref, p_ref, idx_ref, gate_ref, rank_ref, cnt_ref, base_ref = refs
    i = pl.program_id(0)
    h = jnp.dot(_select_rows(a_refs, a_starts, i), w_ref[...], preferred_element_type=F32) + b_ref[...]
    y = _layer_norm(ALPHA * _select_rows(x_refs, x_starts, i) + h, g_ref[...], beta_ref[...])
    o_ref[...] = y
    p_ref[...] = _pack_bf16_pairs(y)
    _route_rows(y, rw_ref, rb_ref, idx_ref, gate_ref, rank_ref, cnt_ref, base_ref, n_experts)


def _out_project_ln_route(a_parts, w, b, x_parts, g, beta, router_w, router_b, tm=ROW_TILE):
    T = sum(a.shape[0] for a in a_parts)
    K, D = w.shape
    E = router_w.shape[1]
    tm = _tile(math.gcd(*[v.shape[0] for v in a_parts + x_parts]), tm)
    a_specs, a_starts = _row_sources(a_parts, tm)
    x_specs, x_starts = _row_sources(x_parts, tm)
    row = lambda i: (i, 0)
    fixed = lambda i: (0, 0)
    once = dict(pipeline_mode=pl.Buffered(1))
    wide = pl.BlockSpec((tm, LANES), row)
    rw = jnp.pad(router_w, ((0, 0), (0, LANES - E)))
    rb = jnp.pad(router_b, (0, LANES - E)).reshape(1, LANES)
    return pl.pallas_call(
        functools.partial(_out_ln_route_kernel, n_a=len(a_parts), n_x=len(x_parts), a_starts=a_starts,
                          x_starts=x_starts, n_experts=E),
        grid=(T // tm,),
        in_specs=(a_specs + [pl.BlockSpec((K, D), fixed, **once), pl.BlockSpec((1, D), fixed)] + x_specs
                  + [pl.BlockSpec((1, D), fixed), pl.BlockSpec((1, D), fixed),
                     pl.BlockSpec((D, LANES), fixed, **once), pl.BlockSpec((1, LANES), fixed)]),
        out_specs=[pl.BlockSpec((tm, D), row), pl.BlockSpec((tm, D // 2), row), wide, wide, wide,
                   pl.BlockSpec((SUBLANES, LANES), fixed)],
        out_shape=[jax.ShapeDtypeStruct((T, D), F32), jax.ShapeDtypeStruct((T, D // 2), jnp.uint32),
                   jax.ShapeDtypeStruct((T, LANES), jnp.int32), jax.ShapeDtypeStruct((T, LANES), F32),
                   jax.ShapeDtypeStruct((T, LANES), jnp.int32), jax.ShapeDtypeStruct((SUBLANES, LANES), F32)],
        scratch_shapes=[pltpu.VMEM((SUBLANES, LANES), F32)],
        compiler_params=_params(("arbitrary",)),
        name="out_project_ln_route",
    )(*a_parts, w, b.reshape(1, D), *x_parts, g.reshape(1, D), beta.reshape(1, D), rw, rb)


def _filter_kernel(z_ref, w_in_ref, b_in_ref, w_mid_ref, b_mid_ref, freq_ref, w_out_ref, delta_ref,
                   k_ref, nrm_ref, *, tl):
    r = pl.program_id(1)
    z = z_ref[...]
    h = jnp.sin(freq_ref[0:1, :] * (_dot3(z, w_in_ref[...]) + b_in_ref[...]))
    for j in range(FILTER_INNER):
        h = jnp.sin(freq_ref[j + 1:j + 2, :] * (_dot3(h, w_mid_ref[j]) + b_mid_ref[j:j + 1, :]))
    out = _dot3(h, w_out_ref[...]) * jnp.exp(-z[:, 0:1] * delta_ref[...])
    row = lax.broadcasted_iota(jnp.int32, out.shape, 0) + r * tl
    out = jnp.where(row == 0, 0.0, out)
    k_ref[0] = out

    @pl.when(r == 0)
    def _():
        nrm_ref[...] = jnp.zeros_like(nrm_ref)

    nrm_ref[0] = nrm_ref[0] + jnp.sum(jnp.abs(out), axis=0, keepdims=True)


def _hyena_filters(L, D, w_in, b_in, w_mid, b_mid, freq, w_out, tl=FILTER_ROW_TILE):
    t = jnp.linspace(0.0, 1.0, L, dtype=F32)[:, None]
    omega = (2.0 * math.pi / L) * jnp.arange(L, dtype=F32)
    bands = jnp.linspace(1e-4, FILTER_BANDS - 1, FILTER_BANDS, dtype=F32)
    phase = omega[:, None] * bands[None, :]
    z = jnp.concatenate([t, jnp.cos(phase), -jnp.sin(phase)], axis=-1)
    z_lag = jnp.concatenate([z[:1], z[1:][::-1], z], axis=0)
    deltas = jnp.abs(jnp.linspace(MIN_DECAY, MAX_DECAY, D, dtype=F32)).reshape(1, D)
    tl = _tile(L, tl)
    n_neg = L // tl
    hid = w_in.shape[1]
    fixed2 = lambda n, r: (0, 0)
    return pl.pallas_call(
        functools.partial(_filter_kernel, tl=tl),
        grid=(HYENA_ORDER, 2 * L // tl),
        in_specs=[pl.BlockSpec((tl, FILTER_EMB), lambda n, r: (r, 0)),
                  pl.BlockSpec((FILTER_EMB, hid), fixed2), pl.BlockSpec((1, hid), fixed2),
                  pl.BlockSpec((FILTER_INNER, hid, hid), lambda n, r: (0, 0, 0)),
                  pl.BlockSpec((FILTER_INNER, hid), fixed2),
                  pl.BlockSpec((FILTER_INNER + 1, hid), fixed2),
                  pl.BlockSpec((hid, D), lambda n, r: (0, 2 * n + (r < n_neg).astype(jnp.int32))),
                  pl.BlockSpec((1, D), fixed2)],
        out_specs=[pl.BlockSpec((1, tl, D), lambda n, r: (n, r, 0)),
                   pl.BlockSpec((1, SUBLANES, D), lambda n, r: (n, 0, 0))],
        out_shape=[jax.ShapeDtypeStruct((HYENA_ORDER, 2 * L, D), F32),
                   jax.ShapeDtypeStruct((HYENA_ORDER, SUBLANES, D), F32)],
        compiler_params=_params(("arbitrary", "arbitrary")),
        name="hyena_filter",
    )(z_lag, w_in, b_in.reshape(1, hid), w_mid, b_mid, freq, w_out, deltas)


def _dft_matrices(bk):
    nfp = (bk + 1 + SUBLANES - 1) // SUBLANES * SUBLANES
    f = jnp.arange(nfp, dtype=jnp.int32)[:, None]
    r = jnp.arange(bk, dtype=jnp.int32)[None, :]
    ang = (2.0 * math.pi / (2 * bk)) * ((f * r) % (2 * bk)).astype(F32)
    live = f <= bk
    cos = jnp.where(live, jnp.cos(ang), 0.0)
    sin = jnp.where(live, jnp.sin(ang), 0.0)
    fwd = jnp.concatenate([cos, -sin], axis=0)
    wgt = jnp.where((f == 0) | (f == bk), 1.0, 2.0) / (2 * bk)
    inv = jnp.concatenate([cos * wgt, -sin * wgt], axis=0).T
    return fwd.astype(BF16), inv.astype(BF16), nfp


def _spectra_kernel(k_ref, nrm_ref, f_ref, c_ref, a_prev, k_prev, *, nfp):
    s = pl.program_id(2)
    kb = k_ref[0]
    a = jnp.dot(f_ref[...], kb.astype(BF16), preferred_element_type=F32)

    @pl.when(s > 0)
    def _():
        row = lax.broadcasted_iota(jnp.int32, a.shape, 0)
        sign = jnp.where((row & 1) == 0, 1.0, -1.0)
        k0 = jnp.where(row < nfp, k_prev[0:1, :], 0.0)
        c = (a + sign * (a_prev[...] - k0)) / nrm_ref[0, 0:1, :]
        width = c_ref.shape[-1]
        for t in range(c_ref.shape[1]):
            c_ref[0, t, 0] = c[:, t * width:(t + 1) * width]

    a_prev[...] = a
    k_prev[...] = kb[0:SUBLANES, :]


def _hyena_spectra(kappa, nrm, fwd, bk, nfp, dt_out, dt=HYENA_CHANNEL_TILE):
    order, two_l, D = kappa.shape
    n_blk = two_l // bk
    dt = max(_tile(D, dt), dt_out)
    per_step = dt // dt_out
    return pl.pallas_call(
        functools.partial(_spectra_kernel, nfp=nfp),
        grid=(order, D // dt, n_blk),
        in_specs=[pl.BlockSpec((1, bk, dt), lambda n, d, s: (n, s, d)),
                  pl.BlockSpec((1, SUBLANES, dt), lambda n, d, s: (n, 0, d)),
                  pl.BlockSpec((2 * nfp, bk), lambda n, d, s: (0, 0))],
        out_specs=pl.BlockSpec((1, per_step, 1, 2 * nfp, dt_out),
                               lambda n, d, s: (n, d, jnp.maximum(s - 1, 0), 0, 0)),
        out_shape=jax.ShapeDtypeStruct((order, D // dt_out, n_blk - 1, 2 * nfp, dt_out), F32),
        scratch_shapes=[pltpu.VMEM((2 * nfp, dt), F32), pltpu.VMEM((SUBLANES, dt), F32)],
        compiler_params=_params(("arbitrary", "arbitrary", "arbitrary")),
        name="hyena_spectra",
    )(kappa, nrm, fwd)


def _short_conv_block(src_ref, w_ref, b_ref, j, nb, bk):
    cur = src_ref[0, j * bk:(j + 1) * bk, :]
    row = lax.broadcasted_iota(jnp.int32, cur.shape, 0)
    if j > 0:
        before = src_ref[0, j * bk - SUBLANES:j * bk, :][SUBLANES - 1:SUBLANES, :]
    else:
        before = jnp.zeros_like(cur[0:1, :])
    if j < nb - 1:
        after = src_ref[0, (j + 1) * bk:(j + 1) * bk + SUBLANES, :][0:1, :]
    else:
        after = jnp.zeros_like(cur[0:1, :])
    up = jnp.where(row == 0, before, pltpu.roll(cur, 1, 0))
    dn = jnp.where(row == bk - 1, after, pltpu.roll(cur, bk - 1, 0))
    return b_ref[...] + up * w_ref[0:1, :] + cur * w_ref[1:2, :] + dn * w_ref[2:3, :]


def _long_conv_kernel(*refs, nb, bk, nfp, conv_input):
    if conv_input:
        (z_ref, zw_ref, zb_ref, g_ref, gw_ref, gb_ref, skip_ref, c_ref, fwd_ref, inv_ref,
         o_ref, zf_ref, y_ref) = refs
    else:
        (z_ref, g_ref, gw_ref, gb_ref, skip_ref, c_ref, fwd_ref, inv_ref, o_ref, zf_ref, y_ref) = refs

    def z_block(j):
        if conv_input:
            return _short_conv_block(z_ref, zw_ref, zb_ref, j, nb, bk)
        return z_ref[0, j * bk:(j + 1) * bk, :]

    for j in range(nb):
        zf_ref[j] = jnp.dot(fwd_ref[...], z_block(j).astype(BF16), preferred_element_type=F32)

    for i in range(nb):
        def chunk(c, carry, i=i):
            r0 = pl.multiple_of(c * SUBLANES, SUBLANES)
            re_rows, im_rows = pl.ds(r0, SUBLANES), pl.ds(nfp + r0, SUBLANES)
            acc_re = acc_im = None
            for j in range(nb):
                m = i - j + nb - 1
                c_re, c_im = c_ref[0, 0, m, re_rows, :], c_ref[0, 0, m, im_rows, :]
                z_re, z_im = zf_ref[j, re_rows, :], zf_ref[j, im_rows, :]
                t_re = c_re * z_re - c_im * z_im
                t_im = c_re * z_im + c_im * z_re
                acc_re = t_re if acc_re is None else acc_re + t_re
                acc_im = t_im if acc_im is None else acc_im + t_im
            y_ref[re_rows, :] = acc_re
            y_ref[im_rows, :] = acc_im
            return carry

        lax.fori_loop(0, nfp // SUBLANES, chunk, 0, unroll=5 if nb >= 8 else 2)
        y = jnp.dot(inv_ref[...], y_ref[...].astype(BF16), preferred_element_type=F32)
        gate = _short_conv_block(g_ref, gw_ref, gb_ref, i, nb, bk)
        o_ref[0, i * bk:(i + 1) * bk, :] = (gate * (y + skip_ref[...] * z_block(i))).astype(o_ref.dtype)


def _long_conv(u3, z3, order, spectra, fwd, inv, conv_w, conv_b, skip, *, b0, nbatch, bk, nfp, dt, out_dtype):
    _, L, three_d = u3.shape
    D = three_d // 3
    nb = L // bk
    n_d = D // dt
    conv_input = z3 is None
    slab = lambda k: (lambda d, b: (b0 + b, 0, k * n_d + d))
    cols = lambda k: (lambda d, b: (0, k * n_d + d))
    seq_block = (1, L, dt)
    own = lambda d, b: (b, 0, d)
    in_specs, args = [], []
    if conv_input:
        in_specs += [pl.BlockSpec(seq_block, slab(0)), pl.BlockSpec((SHORT_CONV, dt), cols(0)),
                     pl.BlockSpec((1, dt), cols(0))]
        args += [u3, conv_w, conv_b]
    else:
        in_specs += [pl.BlockSpec(seq_block, own)]
        args += [z3]
    in_specs += [pl.BlockSpec(seq_block, slab(order + 1)), pl.BlockSpec((SHORT_CONV, dt), cols(order + 1)),
                 pl.BlockSpec((1, dt), cols(order + 1)),
                 pl.BlockSpec((1, dt), lambda d, b: (0, d)),
                 pl.BlockSpec((1, 1, 2 * nb - 1, 2 * nfp, dt), lambda d, b: (order, d, 0, 0, 0),
                              pipeline_mode=pl.Buffered(1)),
                 pl.BlockSpec((2 * nfp, bk), lambda d, b: (0, 0)),
                 pl.BlockSpec((bk, 2 * nfp), lambda d, b: (0, 0))]
    args += [u3, conv_w, conv_b, skip[order:order + 1], spectra, fwd, inv]
    return pl.pallas_call(
        functools.partial(_long_conv_kernel, nb=nb, bk=bk, nfp=nfp, conv_input=conv_input),
        grid=(n_d, nbatch),
        in_specs=in_specs,
        out_specs=pl.BlockSpec(seq_block, own),
        out_shape=jax.ShapeDtypeStruct((nbatch, L, D), out_dtype),
        scratch_shapes=[pltpu.VMEM((nb, 2 * nfp, dt), F32), pltpu.VMEM((2 * nfp, dt), F32)],
        compiler_params=_params(("arbitrary", "arbitrary")),
        name="hyena_long_conv",
    )(*args)


def _hyena_mixer(x_parts, groups, p):
    T, D = sum(x.shape[0] for x in x_parts), x_parts[0].shape[1]
    u = _project(x_parts, p["hy_w_in"].astype(BF16), p["hy_b_in"], F32)
    conv_w, conv_b, skip = p["hy_conv_w"], p["hy_conv_b"].reshape(1, 3 * D), p["hy_filt_skip"]
    outs = []
    for first, nbatch, L in groups:
        bk = min(L, HYENA_BLOCK)
        dt = _tile(D, HYENA_CHANNEL_TILE if L <= HYENA_LONG_SEQ else HYENA_CHANNEL_TILE // 2)
        fwd, inv, nfp = _dft_matrices(bk)
        kappa, nrm = _hyena_filters(L, D, p["hy_filt_w_in"], p["hy_filt_b_in"], p["hy_filt_w_mid"],
                                    p["hy_filt_b_mid"], p["hy_filt_freq"], p["hy_filt_w_out"])
        spectra = _hyena_spectra(kappa, nrm, fwd, bk, nfp, dt)
        kw = dict(b0=first // L, nbatch=nbatch, bk=bk, nfp=nfp, dt=dt)
        u3 = u.reshape(T // L, L, 3 * D)
        z1 = _long_conv(u3, None, 0, spectra, fwd, inv, conv_w, conv_b, skip, out_dtype=F32, **kw)
        z2 = _long_conv(u3, z1, 1, spectra, fwd, inv, conv_w, conv_b, skip, out_dtype=BF16, **kw)
        outs.append(z2.reshape(nbatch * L, D))
    return outs


def _attn_kernel(q_ref, k_ref, v_ref, lam_ref, g_ref, o_ref, *, lam_init, kc):
    dh = DIFF_HEAD_DIM
    lam_dots = jnp.sum(lam_ref[0:2, :] * lam_ref[2:4, :], axis=-1, keepdims=True)
    lam = jnp.exp(lam_dots[0:1, :]) - jnp.exp(lam_dots[1:2, :]) + lam_init
    q = q_ref[...]
    n_chunks = k_ref.shape[0] // kc
    state = [None, None]
    for j in range(n_chunks):
        rows = slice(j * kc, (j + 1) * kc)
        v = v_ref[rows, :]
        for c in range(2):
            s = lax.dot_general(q[:, c * dh:(c + 1) * dh], k_ref[rows, c * dh:(c + 1) * dh],
                                (((1,), (1,)), ((), ())), preferred_element_type=F32)
            top = jnp.max(s, axis=-1, keepdims=True)
            if state[c] is None:
                e = jnp.exp2(s - top)
                state[c] = (top, jnp.sum(e, axis=-1, keepdims=True),
                            jnp.dot(e.astype(BF16), v, preferred_element_type=F32))
            else:
                m_old, l_old, acc = state[c]
                m_new = jnp.maximum(m_old, top)
                shrink = jnp.exp2(m_old - m_new)
                e = jnp.exp2(s - m_new)
                state[c] = (m_new, shrink * l_old + jnp.sum(e, axis=-1, keepdims=True),
                            shrink * acc + jnp.dot(e.astype(BF16), v, preferred_element_type=F32))
    (_, l0, o0), (_, l1, o1) = state
    o = o0 * (1.0 / l0) - o1 * (lam / l1)
    o = o * lax.rsqrt(jnp.mean(o * o, axis=-1, keepdims=True) + LN_EPS) * g_ref[...]
    o_ref[...] = (o * (1.0 - lam_init)).astype(o_ref.dtype)


def _diff_attention(qkv, lam_vecs, subln_g, layer_idx, *, first, nbatch, L, tq, kc):
    T, three_d = qkv.shape
    D = three_d // 3
    hw = 2 * DIFF_HEAD_DIM
    H = D // hw
    tq, kc = _tile(L, tq), _tile(L, kc)
    nq = L // tq
    q0, b0 = first // tq, first // L
    lam_init = 0.8 - 0.6 * math.exp(-0.3 * layer_idx)
    return pl.pallas_call(
        functools.partial(_attn_kernel, lam_init=lam_init, kc=kc),
        grid=(nbatch, H, nq),
        in_specs=[pl.BlockSpec((tq, hw), lambda b, h, i: (q0 + b * nq + i, h)),
                  pl.BlockSpec((L, hw), lambda b, h, i: (b0 + b, H + h)),
                  pl.BlockSpec((L, hw), lambda b, h, i: (b0 + b, 2 * H + h)),
                  pl.BlockSpec((4, DIFF_HEAD_DIM), lambda b, h, i: (0, 0)),
                  pl.BlockSpec((1, hw), lambda b, h, i: (0, 0))],
        out_specs=pl.BlockSpec((tq, hw), lambda b, h, i: (b * nq + i, h)),
        out_shape=jax.ShapeDtypeStruct((nbatch * L, D), BF16),
        compiler_params=_params(("arbitrary", "arbitrary", "arbitrary")),
        name="diff_attention",
    )(qkv, qkv, qkv, lam_vecs, subln_g.reshape(1, hw))


def _attention_mixer(x, groups, p, layer_idx):
    split = groups[1][0] if len(groups) > 1 else x.shape[0]
    l1, l2 = groups[0][2], groups[-1][2]
    qkv = _qkv_project(x, p["da_w_qkv"].astype(BF16), split, l1, l2)
    lam_vecs = jnp.stack([p["da_lambda_q1"], p["da_lambda_q2"], p["da_lambda_k1"], p["da_lambda_k2"]])
    return [_diff_attention(qkv, lam_vecs, p["da_subln_g"], layer_idx, first=first, nbatch=nbatch, L=L,
                            tq=ATTN_Q_TILE * (1 if L > 2048 else 2), kc=ATTN_KEY_CHUNK // (1 if L > 2048 else 2))
            for first, nbatch, L in groups]


def _row_copy(src, src_row, dst, dst_row, sem):
    return pltpu.make_async_copy(src.at[pl.ds(src_row, 1)], dst.at[pl.ds(dst_row, 1)], sem)


def _dispatch_kernel(pend_ref, padded_ref, dest_ref, x_ref, xs_out, zeros, sem, zero_sem, *, tm, rb, n_experts):
    @pl.when(pl.program_id(0) == 0)
    def _():
        zeros[...] = jnp.zeros_like(zeros)

        def last_tile(e):
            first_row = pl.multiple_of(pend_ref[e] - rb, rb)
            return pltpu.make_async_copy(zeros, xs_out.at[pl.ds(first_row, rb)], zero_sem)

        for e in range(n_experts):
            @pl.when(padded_ref[e] > 0)
            def _(e=e):
                last_tile(e).start()

        for e in range(n_experts):
            @pl.when(padded_ref[e] > 0)
            def _(e=e):
                last_tile(e).wait()

    def issue(group, carry):
        base = pl.multiple_of(group * SUBLANES, SUBLANES)
        for u in range(SUBLANES):
            for k in range(TOP_K):
                _row_copy(x_ref, base + u, xs_out, dest_ref[(base + u) * TOP_K + k], sem).start(priority=k % 2)
        return carry

    lax.fori_loop(0, tm // SUBLANES, issue, 0)
    done = xs_out.at[pl.ds(0, TOP_K * tm)]
    pltpu.make_async_copy(done, done, sem).wait()


def _dispatch(x_packed, dest_flat, padded_end, padded, n_rows, rb, tm=DISPATCH_TOKENS):
    T, half = x_packed.shape
    tm = _tile(T, tm)
    return pl.pallas_call(
        functools.partial(_dispatch_kernel, tm=tm, rb=rb, n_experts=padded.shape[0]),
        grid_spec=pltpu.PrefetchScalarGridSpec(
            num_scalar_prefetch=2,
            grid=(T // tm,),
            in_specs=[pl.BlockSpec((tm * TOP_K,), lambda i, pe, pd: (i,), memory_space=pltpu.SMEM),
                      pl.BlockSpec((tm, half), lambda i, pe, pd: (i, 0))],
            out_specs=pl.BlockSpec(memory_space=pl.ANY),
            scratch_shapes=[pltpu.VMEM((rb, half), jnp.uint32), pltpu.SemaphoreType.DMA(()),
                            pltpu.SemaphoreType.DMA(())]),
        out_shape=jax.ShapeDtypeStruct((n_rows, half), jnp.uint32),
        compiler_params=_params(("arbitrary",)),
        name="moe_dispatch",
    )(padded_end, padded, dest_flat, x_packed)


def _expert_kernel(tile_expert_ref, n_used_ref, x_ref, bg_ref, bu_ref, bd_ref, wg_hbm, wu_hbm, wd_hbm, o_ref,
                   wg, wu, wd, stage, sem, *, chunk):
    i = pl.program_id(0)
    expert = tile_expert_ref[i]

    @pl.when(i < n_used_ref[0])
    def _():
        @pl.when((i == 0) | (tile_expert_ref[jnp.maximum(i - 1, 0)] != expert))
        def _():
            pieces = [(src, dst, r) for src, dst in ((wg_hbm, wg), (wu_hbm, wu), (wd_hbm, wd))
                      for r in range(0, dst.shape[0], chunk)]

            def fetch(n):
                src, _, r = pieces[n]
                return pltpu.make_async_copy(src.at[expert, pl.ds(r, chunk)], stage.at[n % 2], sem.at[n % 2])

            fetch(0).start()
            for n, (_, dst, r) in enumerate(pieces):
                if n + 1 < len(pieces):
                    fetch(n + 1).start()
                fetch(n).wait()
                dst[r:r + chunk, :] = stage[n % 2].astype(BF16)

        x = _unpack_bf16_pairs(x_ref[...])
        gate = jnp.dot(x, wg[...], preferred_element_type=F32) + bg_ref[0]
        up = jnp.dot(x, wu[...], preferred_element_type=F32) + bu_ref[0]
        gate = jnp.minimum(gate, SWIGLU_LIMIT)
        up = jnp.clip(up, -SWIGLU_LIMIT, SWIGLU_LIMIT)
        h = gate * (1.0 / (1.0 + jnp.exp(-SWIGLU_ALPHA * gate))) * (up + 1.0)
        y = jnp.dot(h.astype(BF16), wd[...], preferred_element_type=F32) + bd_ref[0]
        o_ref[...] = _pack_bf16_pairs(y)


def _experts(xs, tile_expert, n_used, wg, bg, wu, bu, wd, bd, rb, chunk=EXPERT_STAGE_ROWS):
    P, half = xs.shape
    E, D, F = wg.shape
    assert D == F, "one staging buffer shape serves all three weight matrices"
    n_tiles = P // rb
    chunk = _tile(D, chunk)
    of_expert = lambda i, te, nu: (te[i], 0, 0)
    hbm = pl.BlockSpec(memory_space=pl.ANY)
    return pl.pallas_call(
        functools.partial(_expert_kernel, chunk=chunk),
        grid_spec=pltpu.PrefetchScalarGridSpec(
            num_scalar_prefetch=2,
            grid=(n_tiles,),
            in_specs=[pl.BlockSpec((rb, half), lambda i, te, nu: (jnp.minimum(i, nu[0] - 1), 0)),
                      pl.BlockSpec((1, 1, F), of_expert), pl.BlockSpec((1, 1, F), of_expert),
                      pl.BlockSpec((1, 1, D), of_expert), hbm, hbm, hbm],
            out_specs=pl.BlockSpec((rb, half), lambda i, te, nu: (i, 0)),
            scratch_shapes=[pltpu.VMEM((D, F), BF16), pltpu.VMEM((D, F), BF16), pltpu.VMEM((F, D), BF16),
                            pltpu.VMEM((2, chunk, F), F32), pltpu.SemaphoreType.DMA((2,))]),
        out_shape=jax.ShapeDtypeStruct((P, half), jnp.uint32),
        compiler_params=_params(("arbitrary",)),
        name="moe_experts",
    )(tile_expert, n_used, xs, bg, bu, bd, wg, wu, wd)


def _combine_kernel(dest_ref, dest_next_ref, gate_ref, x_ref, g_ref, beta_ref, ys_hbm, *rest,
                    tc, n_steps, n_first):
    *o_refs, rows, sem = rest
    i = pl.program_id(0)
    slot = lax.rem(i, 2)

    def gather(d_ref, s):
        def issue(group, carry):
            base = pl.multiple_of(group * SUBLANES, SUBLANES)
            for u in range(SUBLANES):
                for k in range(TOP_K):
                    _row_copy(ys_hbm, d_ref[(base + u) * TOP_K + k], rows.at[s, k], base + u,
                              sem.at[s]).start(priority=k % 2)
            return carry
        lax.fori_loop(0, tc // SUBLANES, issue, 0)

    @pl.when(i == 0)
    def _():
        gather(dest_ref, 0)

    @pl.when(i + 1 < n_steps)
    def _():
        gather(dest_next_ref, 1 - slot)

    pltpu.make_async_copy(rows.at[slot], rows.at[slot], sem.at[slot]).wait()
    gates = gate_ref[...]
    mix_hi = mix_lo = None
    for k in range(TOP_K):
        w = rows[slot, k]
        hi = gates[:, k:k + 1] * pltpu.bitcast(w & jnp.uint32(0xFFFF0000), F32)
        lo = gates[:, k:k + 1] * pltpu.bitcast(w << 16, F32)
        mix_hi = hi if mix_hi is None else mix_hi + hi
        mix_lo = lo if mix_lo is None else mix_lo + lo
    mix = jnp.concatenate([mix_hi, mix_lo], axis=1)
    y = _layer_norm(ALPHA * x_ref[...] + mix, g_ref[...], beta_ref[...])
    if n_first is None:
        o_refs[0][...] = y
    else:
        @pl.when(i < n_first)
        def _():
            o_refs[0][...] = y

        @pl.when(i >= n_first)
        def _():
            o_refs[1][...] = y


def _combine_ln(ys, dest_flat, gates, x, g, beta, split=None, tc=COMBINE_TOKENS):
    T, D = x.shape
    tc = _tile(T if split is None else math.gcd(split, T - split), tc)
    n_steps = T // tc
    row = lambda i: (i, 0)
    fixed = lambda i: (0, 0)
    tile = (tc, D)
    if split is None:
        n_first = None
        out_specs = pl.BlockSpec(tile, row)
        out_shape = jax.ShapeDtypeStruct((T, D), F32)
    else:
        n_first = split // tc
        out_specs = [pl.BlockSpec(tile, lambda i: (jnp.minimum(i, n_first - 1), 0)),
                     pl.BlockSpec(tile, lambda i: (jnp.maximum(i - n_first, 0), 0))]
        out_shape = [jax.ShapeDtypeStruct((split, D), F32), jax.ShapeDtypeStruct((T - split, D), F32)]
    return pl.pallas_call(
        functools.partial(_combine_kernel, tc=tc, n_steps=n_steps, n_first=n_first),
        grid=(n_steps,),
        in_specs=[pl.BlockSpec((tc * TOP_K,), lambda i: (i,), memory_space=pltpu.SMEM),
                  pl.BlockSpec((tc * TOP_K,), lambda i: (jnp.minimum(i + 1, n_steps - 1),),
                               memory_space=pltpu.SMEM),
                  pl.BlockSpec((tc, LANES), row), pl.BlockSpec(tile, row),
                  pl.BlockSpec((1, D), fixed), pl.BlockSpec((1, D), fixed),
                  pl.BlockSpec(memory_space=pl.ANY)],
        out_specs=out_specs,
        out_shape=out_shape,
        scratch_shapes=[pltpu.VMEM((2, TOP_K, tc, D // 2), jnp.uint32), pltpu.SemaphoreType.DMA((2,))],
        compiler_params=_params(("arbitrary",)),
        name="moe_combine_ln",
    )(dest_flat, dest_flat, gates, x, g.reshape(1, D), beta.reshape(1, D), ys)


def _expert_stacks(p):
    n_layers, E, D, F = p["moe_w_gate"].shape
    stack = lambda w: w.reshape((n_layers * E,) + w.shape[2:])
    bias = lambda b: b.reshape(n_layers * E, 1, b.shape[-1])
    return (stack(p["moe_w_gate"]), bias(p["moe_b_gate"]), stack(p["moe_w_up"]), bias(p["moe_b_up"]),
            stack(p["moe_w_down"]), bias(p["moe_b_down"]))


def _moe_ln(x, x_packed, routing, p, stacks, i, rb, split=None):
    T, D = x.shape
    E = p["moe_router_w"].shape[-1]
    idx, gates, rank, counts = routing
    counts = counts[0, :E].astype(jnp.int32)
    padded = (counts + rb - 1) // rb * rb
    padded_end = jnp.cumsum(padded)
    padded_start = padded_end - padded
    n_tiles = (T * TOP_K) // rb + E
    tile_row = jnp.arange(n_tiles, dtype=jnp.int32) * rb
    tile_expert = jnp.minimum(jnp.sum(padded_end[None, :] <= tile_row[:, None], axis=1), E - 1).astype(jnp.int32)
    n_used = (padded_end[-1:] // rb).astype(jnp.int32)
    dest = (padded_start[idx[:, :TOP_K]] + rank[:, :TOP_K]).reshape(T * TOP_K)
    xs = _dispatch(x_packed, dest, padded_end.astype(jnp.int32), padded, n_tiles * rb, rb)
    ys = _experts(xs, tile_expert + i * E, n_used, *stacks, rb)
    return _combine_ln(ys, dest, gates, x, p["ln2_g"][i], p["ln2_b"][i], split)


def _encoder(x_parts, groups, p, rb):
    stacks = _expert_stacks(p)
    D = x_parts[0].shape[1]
    for i in range(DEPTH):
        j = i // 2
        if i % 2 == 0:
            pj = {k: v[j] for k, v in p.items() if k.startswith("hy_")}
            a_parts = _hyena_mixer(x_parts, groups, pj)
            w_o, b_o = pj["hy_w_out"], pj["hy_b_out"]
        else:
            pj = {k: v[j] for k, v in p.items() if k.startswith("da_")}
            a_parts = _attention_mixer(x_parts[0], groups, pj, i)
            w_o, b_o = pj["da_w_o"], jnp.zeros((D,), F32)
        x, x_packed, *routing = _out_project_ln_route(a_parts, w_o.astype(BF16), b_o, x_parts, p["ln1_g"][i],
                                                      p["ln1_b"][i], p["moe_router_w"][i], p["moe_router_b"][i])
        last = i == DEPTH - 1
        y = _moe_ln(x, x_packed, routing, p, stacks, i, rb, groups[1][0] if last and len(groups) > 1 else None)
        x_parts = list(y) if isinstance(y, (list, tuple)) else [y]
    return x_parts


def _forward(x_prompt, x_sample, p, rb=EXPERT_ROW_TILE):
    B1, L1, D = x_prompt.shape
    B2, L2, _ = x_sample.shape
    T1, T2 = B1 * L1, B2 * L2
    assert T1 % L2 == 0, "sample sequences must start on a multiple of their length"
    y1, y2 = _encoder([x_prompt.reshape(T1, D), x_sample.reshape(T2, D)], [(0, B1, L1), (T1, B2, L2)], p, rb)
    return y1.reshape(B1, L1, D), y2.reshape(B2, L2, D)


def kernel(x_prompt, x_sample, hy_w_in, hy_b_in, hy_conv_w, hy_conv_b, hy_filt_w_in, hy_filt_b_in, hy_filt_w_mid, hy_filt_b_mid, hy_filt_freq, hy_filt_w_out, hy_filt_skip, hy_w_out, hy_b_out, da_w_qkv, da_lambda_q1, da_lambda_k1, da_lambda_q2, da_lambda_k2, da_subln_g, da_w_o, ln1_g, ln1_b, moe_router_w, moe_router_b, moe_w_gate, moe_b_gate, moe_w_up, moe_b_up, moe_w_down, moe_b_down, ln2_g, ln2_b):
    p = dict(hy_w_in=hy_w_in, hy_b_in=hy_b_in, hy_conv_w=hy_conv_w, hy_conv_b=hy_conv_b,
             hy_filt_w_in=hy_filt_w_in, hy_filt_b_in=hy_filt_b_in, hy_filt_w_mid=hy_filt_w_mid,
             hy_filt_b_mid=hy_filt_b_mid, hy_filt_freq=hy_filt_freq, hy_filt_w_out=hy_filt_w_out,
             hy_filt_skip=hy_filt_skip, hy_w_out=hy_w_out, hy_b_out=hy_b_out,
             da_w_qkv=da_w_qkv, da_lambda_q1=da_lambda_q1, da_lambda_k1=da_lambda_k1,
             da_lambda_q2=da_lambda_q2, da_lambda_k2=da_lambda_k2, da_subln_g=da_subln_g, da_w_o=da_w_o,
             ln1_g=ln1_g, ln1_b=ln1_b, moe_router_w=moe_router_w, moe_router_b=moe_router_b,
             moe_w_gate=moe_w_gate, moe_b_gate=moe_b_gate, moe_w_up=moe_w_up, moe_b_up=moe_b_up,
             moe_w_down=moe_w_down, moe_b_down=moe_b_down, ln2_g=ln2_g, ln2_b=ln2_b)
    return _forward(x_prompt, x_sample, p)
```
